```python
import jax, jax.numpy as jnp
from jax import lax
import numpy as np

D_MODEL = 1024
BATCH = 4
SEQ = 8192
DEPTH = 1

CHUNK = 64
Q_BLOCK = 128
SB_HEADS = 16
SB_HEAD_DIM = 64
SB_WIDTH = SB_HEADS * SB_HEAD_DIM
RET_HEADS = 4
RET_QK_DIM = 256
RET_V_DIM = 512
RET_QK_WIDTH = RET_HEADS * RET_QK_DIM
RET_V_WIDTH = RET_HEADS * RET_V_DIM
ROPE_BASE = 10000.0
N_EXPERTS = 32
TOP_K = 4
D_FF = 1024
SWIGLU_LIMIT = 7.0
SWIGLU_ALPHA = 1.702
EXPERT_BLOCK = 256
LN_EPS = 1e-5
DEEPNORM_ALPHA = (2.0 * DEPTH) ** 0.25
DEEPNORM_BETA = (8.0 * DEPTH) ** -0.25
IN_SIZES = (SB_WIDTH, SB_WIDTH, SB_WIDTH, RET_QK_WIDTH, RET_QK_WIDTH, RET_V_WIDTH, RET_V_WIDTH, D_MODEL, D_MODEL)
IN_WIDTH = 3 * SB_WIDTH + 2 * RET_QK_WIDTH + 2 * RET_V_WIDTH + 2 * D_MODEL

kernel_name = 'hybrid_stickbreak_retention_moe_deepnorm'


def layer_norm(x, g, b):
    xf = x.astype(jnp.float32)
    mu = jnp.mean(xf, axis=-1, keepdims=True)
    var = jnp.mean(jnp.square(xf - mu), axis=-1, keepdims=True)
    return ((xf - mu) * lax.rsqrt(var + LN_EPS)).astype(x.dtype) * g + b


def group_norm_heads(y):
    yf = y.astype(jnp.float32)
    mu = jnp.mean(yf, axis=-1, keepdims=True)
    var = jnp.mean(jnp.square(yf - mu), axis=-1, keepdims=True)
    return ((yf - mu) * lax.rsqrt(var + LN_EPS)).astype(y.dtype)


def split_heads(t, h):
    b, s, w = t.shape
    return t.reshape(b, s, h, w // h).transpose(0, 2, 1, 3)


def merge_heads(t):
    b, h, s, d = t.shape
    return t.transpose(0, 2, 1, 3).reshape(b, s, h * d)


def rotary(x, pos):
    half = x.shape[-1] // 2
    inv_freq = ROPE_BASE ** (-jnp.arange(half, dtype=jnp.float32) / half)
    ang = pos[:, None] * inv_freq[None, :]
    cos = jnp.cos(ang).astype(x.dtype)
    sin = jnp.sin(ang).astype(x.dtype)
    x1, x2 = x[..., :half], x[..., half:]
    return jnp.concatenate([x1 * cos - x2 * sin, x1 * sin + x2 * cos], axis=-1)


def stick_breaking_attention(q, k, v):
    s_len, d = q.shape[2], q.shape[3]
    scale = d ** -0.5
    outs = []
    for blk in range(s_len // Q_BLOCK):
        q0 = blk * Q_BLOCK
        q1 = q0 + Q_BLOCK
        qb = q[:, :, q0:q1]
        kb = k[:, :, :q1]
        vb = v[:, :, :q1]
        z = jnp.einsum('bhqd,bhkd->bhqk', qb, kb).astype(jnp.float32) * scale
        t_idx = q0 + jnp.arange(Q_BLOCK)[:, None]
        s_idx = jnp.arange(q1)[None, :]
        strict = s_idx < t_idx
        log_one_minus_beta = jnp.where(strict, jax.nn.log_sigmoid(-z), 0.0)
        later = lax.cumsum(log_one_minus_beta, axis=3, reverse=True) - log_one_minus_beta
        a = jnp.where(strict, jnp.exp(jax.nn.log_sigmoid(z) + later), 0.0)
        outs.append(jnp.einsum('bhqk,bhkd->bhqd', a.astype(v.dtype), vb))
    return jnp.concatenate(outs, axis=2)


def retention_chunkwise(q, k, v):
    b, h, s_len, dk = q.shape
    dv = v.shape[-1]
    n_chunks = s_len // CHUNK
    log_gamma = jnp.log(1.0 - 2.0 ** (-5.0 - jnp.arange(h, dtype=jnp.float32)))
    i = jnp.arange(CHUNK, dtype=jnp.float32)
    intra = jnp.exp(log_gamma[:, None, None] * jnp.abs(i[:, None] - i[None, :]))
    q_decay = jnp.exp(log_gamma[:, None] * (i + 1.0))
    k_decay = jnp.exp(log_gamma[:, None] * (CHUNK - 1.0 - i))
    chunk_decay = jnp.exp(log_gamma * CHUNK)
    q = q * (dk ** -0.5)

    def to_chunks(t):
        return t.reshape(b, h, n_chunks, CHUNK, t.shape[-1]).transpose(2, 0, 1, 3, 4)

    def step(state, qkv):
        qc, kc, vc = qkv
        scores = jnp.einsum('bhid,bhjd->bhij', qc, kc) * intra.astype(qc.dtype)
        y_intra = jnp.einsum('bhij,bhjv->bhiv', scores, vc).astype(jnp.float32)
        y_inter = jnp.einsum('bhid,bhdv->bhiv',
                             qc.astype(jnp.float32) * q_decay[None, :, :, None], state)
        new_state = state * chunk_decay[None, :, None, None] + jnp.einsum(
            'bhjd,bhjv->bhdv', kc.astype(jnp.float32) * k_decay[None, :, :, None],
            vc.astype(jnp.float32))
        return new_state, (y_intra + y_inter).astype(vc.dtype)

    state0 = jnp.zeros((b, h, dk, dv), jnp.float32)
    _, ys = lax.scan(step, state0, (to_chunks(q), to_chunks(k), to_chunks(v)))
    return ys.transpose(1, 2, 0, 3, 4).reshape(b, h, s_len, dv)


def hybrid_mixer(x, w_in, w_branch_sb, w_branch_ret, w_out):
    s_len = x.shape[1]
    points = [int(c) for c in np.cumsum(IN_SIZES)[:-1]]
    q_sb, k_sb, v_sb, q_r, k_r, v_r, g_r, gate_sb, gate_ret = jnp.split(x @ w_in, points, axis=-1)
    y_sb = stick_breaking_attention(split_heads(q_sb, SB_HEADS), split_heads(k_sb, SB_HEADS),
                                    split_heads(v_sb, SB_HEADS))
    branch_sb = merge_heads(y_sb) @ w_branch_sb
    pos = jnp.arange(s_len, dtype=jnp.float32)
    y_r = retention_chunkwise(rotary(split_heads(q_r, RET_HEADS), pos),
                              rotary(split_heads(k_r, RET_HEADS), pos),
                              split_heads(v_r, RET_HEADS))
    branch_ret = (jax.nn.silu(g_r) * merge_heads(group_norm_heads(y_r))) @ w_branch_ret
    merged = jax.nn.sigmoid(gate_sb) * branch_sb + jax.nn.sigmoid(gate_ret) * branch_ret
    return merged @ w_out


def moe_ffn(x, w_router, b_router, w_gate_up, b_gate_up, w_down, b_down):
    b, s_len, d = x.shape
    xf = x.reshape(-1, d)
    n_tok = xf.shape[0]
    logits = (xf @ w_router + b_router).astype(jnp.float32)
    top_val, top_idx = lax.top_k(logits, TOP_K)
    top_w = jax.nn.softmax(top_val, axis=-1).astype(x.dtype)
    n_assign = n_tok * TOP_K
    expert_flat = top_idx.reshape(-1)
    token_flat = jnp.arange(n_assign, dtype=jnp.int32) // TOP_K
    weight_flat = top_w.reshape(-1)
    order = jnp.argsort(expert_flat)
    sorted_e = expert_flat[order]
    counts = jnp.bincount(expert_flat, length=N_EXPERTS)
    starts = jnp.cumsum(counts) - counts
    padded = (counts + EXPERT_BLOCK - 1) // EXPERT_BLOCK * EXPERT_BLOCK
    padded_ends = jnp.cumsum(padded)
    padded_starts = padded_ends - padded
    dest = padded_starts[sorted_e] + jnp.arange(n_assign) - starts[sorted_e]
    n_rows = -(-n_assign // EXPERT_BLOCK) * EXPERT_BLOCK + N_EXPERTS * EXPERT_BLOCK
    n_blocks = n_rows // EXPERT_BLOCK
    row_token = jnp.zeros((n_rows,), jnp.int32).at[dest].set(token_flat[order])
    row_weight = jnp.zeros((n_rows,), x.dtype).at[dest].set(weight_flat[order])
    block_expert = jnp.minimum(
        jnp.searchsorted(padded_ends, jnp.arange(n_blocks) * EXPERT_BLOCK, side='right'),
        N_EXPERTS - 1)

    def run_block(args):
        e, tok, w = args
        hgu = xf[tok] @ w_gate_up[e] + b_gate_up[e]
        gate = jnp.minimum(hgu[:, :D_FF], SWIGLU_LIMIT)
        up = jnp.clip(hgu[:, D_FF:], -SWIGLU_LIMIT, SWIGLU_LIMIT)
        glu = gate * jax.nn.sigmoid(gate * SWIGLU_ALPHA)
        out = ((up + 1.0) * glu) @ w_down[e] + b_down[e]
        return out * w[:, None]

    rows = lax.map(run_block, (block_expert, row_token.reshape(n_blocks, EXPERT_BLOCK),
                               row_weight.reshape(n_blocks, EXPERT_BLOCK)))
    y = jax.ops.segment_sum(rows.reshape(n_rows, d), row_token, num_segments=n_tok)
    return y.reshape(b, s_len, d)


def setup_inputs(seed: int = 0) -> dict:
    key = jax.random.key(seed)
    ks = jax.random.split(key, 20)
    f32 = jnp.float32

    def nrm(k, shape, scale):
        return jax.random.normal(k, shape, f32) * scale

    x = jax.random.normal(ks[0], (BATCH, SEQ, D_MODEL), f32)
    col_scale = jnp.concatenate([
        jnp.full((sz,), DEEPNORM_BETA if idx in (2, 5) else 1.0, f32)
        for idx, sz in enumerate(IN_SIZES)])
    w_in = nrm(ks[1], (DEPTH, D_MODEL, IN_WIDTH), D_MODEL ** -0.5) * col_scale
    w_branch_sb = nrm(ks[2], (DEPTH, SB_WIDTH, D_MODEL), SB_WIDTH ** -0.5 * DEEPNORM_BETA)
    w_branch_ret = nrm(ks[3], (DEPTH, RET_V_WIDTH, D_MODEL), RET_V_WIDTH ** -0.5 * DEEPNORM_BETA)
    w_out = nrm(ks[4], (DEPTH, D_MODEL, D_MODEL), D_MODEL ** -0.5 * DEEPNORM_BETA)
    ln1_g = 1.0 + nrm(ks[5], (DEPTH, D_MODEL), 0.02)
    ln1_b = nrm(ks[6], (DEPTH, D_MODEL), 0.02)
    w_router = nrm(ks[7], (DEPTH, D_MODEL, N_EXPERTS), D_MODEL ** -0.5)
    b_router = nrm(ks[8], (DEPTH, N_EXPERTS), 0.01)
    w_gate_up = nrm(ks[9], (DEPTH, N_EXPERTS, D_MODEL, 2 * D_FF), D_MODEL ** -0.5 * DEEPNORM_BETA)
    b_gate_up = nrm(ks[10], (DEPTH, N_EXPERTS, 2 * D_FF), 0.01)
    w_down = nrm(ks[11], (DEPTH, N_EXPERTS, D_FF, D_MODEL), D_FF ** -0.5 * DEEPNORM_BETA)
    b_down = nrm(ks[12], (DEPTH, N_EXPERTS, D_MODEL), 0.01)
    ln2_g = 1.0 + nrm(ks[13], (DEPTH, D_MODEL), 0.02)
    ln2_b = nrm(ks[14], (DEPTH, D_MODEL), 0.02)
    return {'x': x, 'w_in': w_in, 'w_branch_sb': w_branch_sb, 'w_branch_ret': w_branch_ret,
            'w_out': w_out, 'ln1_g': ln1_g, 'ln1_b': ln1_b, 'w_router': w_router,
            'b_router': b_router, 'w_gate_up': w_gate_up, 'b_gate_up': b_gate_up,
            'w_down': w_down, 'b_down': b_down, 'ln2_g': ln2_g, 'ln2_b': ln2_b}


def reference(x, w_in, w_branch_sb, w_branch_ret, w_out, ln1_g, ln1_b, w_router, b_router,
              w_gate_up, b_gate_up, w_down, b_down, ln2_g, ln2_b):
    for l in range(DEPTH):
        mix = hybrid_mixer(x, w_in[l], w_branch_sb[l], w_branch_ret[l], w_out[l])
        x = layer_norm(DEEPNORM_ALPHA * x + mix, ln1_g[l], ln1_b[l])
        ffn = moe_ffn(x, w_router[l], b_router[l], w_gate_up[l], b_gate_up[l], w_down[l], b_down[l])
        x = layer_norm(DEEPNORM_ALPHA * x + ffn, ln2_g[l], ln2_b[l])
    return x
```

```python
import functools

import numpy as np
import jax
import jax.numpy as jnp
from jax import lax
from jax.experimental import pallas as pl
from jax.experimental.pallas import tpu as pltpu

F32 = jnp.float32
BF16 = jnp.bfloat16

SB_HEADS = 16
SB_HEAD_DIM = 64
RET_HEADS = 4
RET_QK_DIM = 256
RET_V_DIM = 512
RET_CHUNK = 64
ROPE_BASE = 10000.0
N_EXPERTS = 32
TOP_K = 4
EXPERT_BLOCK = 256
SWIGLU_LIMIT = 7.0
SWIGLU_ALPHA = 1.702
LN_EPS = 1e-5
DEPTH = 1
DEEPNORM_ALPHA = (2.0 * DEPTH) ** 0.25

LANES = 128
VMEM_LIMIT = 56 * 1024 * 1024

EXP_UNDERFLOW = -104.0


def _cparams(sem):
    return pltpu.CompilerParams(dimension_semantics=sem, vmem_limit_bytes=VMEM_LIMIT)


def _proj_kernel(x_ref, w_ref, o_ref, xb_ref):
    @pl.when(pl.program_id(1) == 0)
    def _():
        xb_ref[...] = x_ref[...].astype(BF16)

    o_ref[...] = jnp.dot(xb_ref[...], w_ref[...],
                         preferred_element_type=F32).astype(o_ref.dtype)


def _in_proj(x2, w_bf, tm, tn):
    n, d = x2.shape
    width = w_bf.shape[1]
    return pl.pallas_call(
        _proj_kernel,
        out_shape=jax.ShapeDtypeStruct((n, width), BF16),
        grid=(n // tm, width // tn),
        in_specs=[pl.BlockSpec((tm, d), lambda i, j: (i, 0)),
                  pl.BlockSpec((d, tn), lambda i, j: (0, j))],
        out_specs=pl.BlockSpec((tm, tn), lambda i, j: (i, j)),
        scratch_shapes=[pltpu.VMEM((tm, d), BF16)],
        compiler_params=_cparams(("parallel", "arbitrary")),
        name="in_proj",
    )(x2, w_bf)


def _sb_kernel(q_ref, k_ref, v_ref, o_ref, acc_ref, carry_ref, *, tq):
    qi = pl.program_id(2)
    lane = lax.broadcasted_iota(jnp.int32, (tq, LANES), 1)
    q = q_ref[...] * jnp.asarray(SB_HEAD_DIM ** -0.5, BF16)
    zero = jnp.zeros_like(q)
    q_heads = (jnp.where(lane < SB_HEAD_DIM, q, zero), jnp.where(lane >= SB_HEAD_DIM, q, zero))
    row = lax.broadcasted_iota(jnp.int32, (tq, tq), 0)
    col = lax.broadcasted_iota(jnp.int32, (tq, tq), 1)
    suffix = (row >= col).astype(BF16)
    strict = col < row

    acc_ref[...] = jnp.zeros_like(acc_ref)
    carry_ref[...] = jnp.zeros_like(carry_ref)

    def block(kb, diag):
        start = pl.multiple_of(kb * tq, tq)
        kblk = k_ref[pl.ds(start, tq), :]
        vblk = v_ref[pl.ds(start, tq), :]
        for h in range(2):
            z = lax.dot_general(q_heads[h], kblk, (((1,), (1,)), ((), ())),
                                preferred_element_type=F32)
            lsm = jnp.minimum(-z, 0.0) - jnp.log(1.0 + jnp.exp(-jnp.abs(z)))
            if diag:
                lsm = jnp.where(strict, lsm, 0.0)
            hi = lsm.astype(BF16)
            lo = (lsm - hi.astype(F32)).astype(BF16)
            cum = (jnp.dot(hi, suffix, preferred_element_type=F32)
                   + jnp.dot(lo, suffix, preferred_element_type=F32))
            carry = carry_ref[h]
            a = jnp.exp(z + cum + carry)
            if diag:
                a = jnp.where(strict, a, 0.0)
            acc_ref[h] += jnp.dot(a.astype(BF16), vblk, preferred_element_type=F32)
            carry_ref[h] = carry + cum[:, 0:1]

    def alive():
        return jnp.max(carry_ref[...]) >= EXP_UNDERFLOW

    block(qi, True)

    def cond(c):
        kb, live = c
        return jnp.logical_and(kb >= 0, live)

    def body(c):
        kb, _ = c
        block(kb, False)
        return kb - 1, alive()

    lax.while_loop(cond, body, (qi - 1, alive()))
    o_ref[...] = jnp.where(lane < SB_HEAD_DIM, acc_ref[0], acc_ref[1]).astype(o_ref.dtype)


def _sb_attention(proj3, tq):
    b, s, _ = proj3.shape
    pairs = SB_HEADS * SB_HEAD_DIM // LANES
    return pl.pallas_call(
        functools.partial(_sb_kernel, tq=tq),
        out_shape=jax.ShapeDtypeStruct((b, s, pairs * LANES), BF16),
        grid=(b, pairs, s // tq),
        in_specs=[pl.BlockSpec((None, tq, LANES), lambda bi, p, i: (bi, i, p)),
                  pl.BlockSpec((None, s, LANES), lambda bi, p, i: (bi, 0, pairs + p)),
                  pl.BlockSpec((None, s, LANES), lambda bi, p, i: (bi, 0, 2 * pairs + p))],
        out_specs=pl.BlockSpec((None, tq, LANES), lambda bi, p, i: (bi, i, p)),
        scratch_shapes=[pltpu.VMEM((2, tq, LANES), F32), pltpu.VMEM((2, tq, 1), F32)],
        compiler_params=_cparams(("parallel", "parallel", "arbitrary")),
        name="sb_attn",
    )(proj3, proj3, proj3)


def _ret_kernel(lg_ref, q_ref, k_ref, v_ref, g_ref, cos_ref, sin_ref, o_ref,
                state_ref, mask_ref, qdec_ref, kdec_ref, *, tb):
    h = pl.program_id(1)
    si = pl.program_id(2)
    lg = lg_ref[h]
    half = RET_QK_DIM // 2

    @pl.when(si == 0)
    def _():
        state_ref[...] = jnp.zeros_like(state_ref)
        i = lax.broadcasted_iota(jnp.int32, (tb, tb), 0)
        j = lax.broadcasted_iota(jnp.int32, (tb, tb), 1)
        dist = jnp.abs(i - j).astype(F32)
        visible = (j // RET_CHUNK) <= (i // RET_CHUNK)
        mask_ref[...] = jnp.where(visible, jnp.exp(lg * dist), 0.0)
        r = lax.broadcasted_iota(jnp.int32, (tb, 1), 0).astype(F32)
        qdec_ref[...] = jnp.exp(lg * (r + 1.0))
        kdec_ref[...] = jnp.exp(lg * (tb - 1.0 - r))

    cos = cos_ref[...]
    sin = sin_ref[...]

    def rot(t):
        t1 = t[:, :half]
        t2 = t[:, half:]
        return jnp.concatenate([t1 * cos - t2 * sin, t1 * sin + t2 * cos], axis=-1)

    qr = rot(q_ref[...].astype(F32)) * (RET_QK_DIM ** -0.5)
    kr = rot(k_ref[...].astype(F32))
    v = v_ref[...]
    scores = lax.dot_general(qr.astype(BF16), kr.astype(BF16), (((1,), (1,)), ((), ())),
                             preferred_element_type=F32) * mask_ref[...]
    state = state_ref[...]
    y = (jnp.dot(scores.astype(BF16), v, preferred_element_type=F32)
         + jnp.dot((qr * qdec_ref[...]).astype(BF16), state.astype(BF16),
                   preferred_element_type=F32))
    kd = (kr * kdec_ref[...]).astype(BF16)
    state_ref[...] = state * jnp.exp(lg * tb) + lax.dot_general(
        kd, v, (((0,), (0,)), ((), ())), preferred_element_type=F32)

    mu = jnp.mean(y, axis=-1, keepdims=True)
    yc = y - mu
    var = jnp.mean(yc * yc, axis=-1, keepdims=True)
    yn = yc * lax.rsqrt(var + LN_EPS)
    g = g_ref[...].astype(F32)
    o_ref[...] = (g * jax.nn.sigmoid(g) * yn).astype(o_ref.dtype)


def _retention(proj3, cos, sin, tb, off_q, off_k, off_v, off_g):
    b, s, _ = proj3.shape
    log_gamma = jnp.asarray(np.log(1.0 - 2.0 ** (-5.0 - np.arange(RET_HEADS))), F32)
    qb, kb = off_q // RET_QK_DIM, off_k // RET_QK_DIM
    vb, gb = off_v // RET_V_DIM, off_g // RET_V_DIM
    half = RET_QK_DIM // 2
    grid_spec = pltpu.PrefetchScalarGridSpec(
        num_scalar_prefetch=1,
        grid=(b, RET_HEADS, s // tb),
        in_specs=[pl.BlockSpec((None, tb, RET_QK_DIM), lambda bi, h, i, lg: (bi, i, qb + h)),
                  pl.BlockSpec((None, tb, RET_QK_DIM), lambda bi, h, i, lg: (bi, i, kb + h)),
                  pl.BlockSpec((None, tb, RET_V_DIM), lambda bi, h, i, lg: (bi, i, vb + h)),
                  pl.BlockSpec((None, tb, RET_V_DIM), lambda bi, h, i, lg: (bi, i, gb + h)),
                  pl.BlockSpec((tb, half), lambda bi, h, i, lg: (i, 0)),
                  pl.BlockSpec((tb, half), lambda bi, h, i, lg: (i, 0))],
        out_specs=pl.BlockSpec((None, tb, RET_V_DIM), lambda bi, h, i, lg: (bi, i, h)),
        scratch_shapes=[pltpu.VMEM((RET_QK_DIM, RET_V_DIM), F32),
                        pltpu.VMEM((tb, tb), F32),
                        pltpu.VMEM((tb, 1), F32),
                        pltpu.VMEM((tb, 1), F32)])
    return pl.pallas_call(
        functools.partial(_ret_kernel, tb=tb),
        out_shape=jax.ShapeDtypeStruct((b, s, RET_HEADS * RET_V_DIM), BF16),
        grid_spec=grid_spec,
        compiler_params=_cparams(("parallel", "parallel", "arbitrary")),
        name="retention",
    )(log_gamma, proj3, proj3, proj3, proj3, cos, sin)


def _layer_norm(h, g, b):
    mu = jnp.mean(h, axis=-1, keepdims=True)
    hc = h - mu
    var = jnp.mean(hc * hc, axis=-1, keepdims=True)
    return hc * lax.rsqrt(var + LN_EPS) * g + b


def _merge_kernel(ysb_ref, yret_ref, gsb_ref, gret_ref, x_ref, wsb_ref, wret_ref, wout_ref,
                  g1_ref, b1_ref, wr_ref, br_ref,
                  x1_ref, x1p_ref, route_ref, topw_ref, cnt_ref, run_ref, *, tm):
    i = pl.program_id(0)

    @pl.when(i == 0)
    def _():
        run_ref[...] = jnp.zeros_like(run_ref)

    bsb = jnp.dot(ysb_ref[...], wsb_ref[...], preferred_element_type=F32)
    bret = jnp.dot(yret_ref[...], wret_ref[...], preferred_element_type=F32)
    merged = (jax.nn.sigmoid(gsb_ref[...].astype(F32)) * bsb
              + jax.nn.sigmoid(gret_ref[...].astype(F32)) * bret)
    mix = jnp.dot(merged.astype(BF16), wout_ref[...], preferred_element_type=F32)
    x1 = _layer_norm(DEEPNORM_ALPHA * x_ref[...] + mix, g1_ref[...], b1_ref[...])
    x1_ref[...] = x1

    half = x1.shape[1] // 2
    left = lax.bitcast_convert_type(x1[:, :half].astype(BF16).astype(F32), jnp.uint32)
    right = lax.bitcast_convert_type(x1[:, half:].astype(BF16).astype(F32), jnp.uint32)
    x1p_ref[...] = left | (right >> 16)

    logits = jnp.dot(x1, wr_ref[...], preferred_element_type=F32,
                     precision=lax.Precision.HIGHEST) + br_ref[...]
    lane = lax.broadcasted_iota(jnp.int32, (tm, LANES), 1)
    lane_f = lane.astype(F32)
    neg = jnp.asarray(-jnp.inf, F32)
    work = jnp.where(lane < N_EXPERTS, logits, neg)
    vals, idxs = [], []
    sel = jnp.zeros((tm, LANES), F32)
    for _ in range(TOP_K):
        m = jnp.max(work, axis=-1, keepdims=True)
        idx = jnp.min(jnp.where(work == m, lane_f, float(LANES)), axis=-1, keepdims=True)
        hit = lane_f == idx
        sel = jnp.where(hit, 1.0, sel)
        work = jnp.where(hit, neg, work)
        vals.append(m)
        idxs.append(idx)
    exps = [jnp.exp(v - vals[0]) for v in vals]
    denom = exps[0] + exps[1] + exps[2] + exps[3]
    weights = [e / denom for e in exps]

    r = lax.broadcasted_iota(jnp.int32, (tm, tm), 0)
    c = lax.broadcasted_iota(jnp.int32, (tm, tm), 1)
    before = (c < r).astype(BF16)
    rank_mat = jnp.dot(before, sel.astype(BF16), preferred_element_type=F32) + run_ref[...]
    route = jnp.zeros((tm, LANES), F32)
    topw = jnp.zeros((tm, LANES), F32)
    for k in range(TOP_K):
        rank_k = jnp.sum(jnp.where(lane_f == idxs[k], rank_mat, 0.0), axis=-1, keepdims=True)
        route = jnp.where(lane == k, idxs[k], route)
        route = jnp.where(lane == TOP_K + k, rank_k, route)
        topw = jnp.where(lane == k, weights[k], topw)
    route_ref[...] = route.astype(jnp.int32)
    topw_ref[...] = topw
    run = run_ref[...] + jnp.sum(sel, axis=0, keepdims=True)
    run_ref[...] = run
    cnt_ref[...] = jnp.broadcast_to(run, cnt_ref.shape).astype(jnp.int32)


def _merge_route(y_sb, y_ret, proj, x2, wsb, wret, wout, g1, b1, wr, br, tm, off_gsb, off_gret):
    n, d = x2.shape
    const = lambda i: (0, 0)
    row = lambda i: (i, 0)
    return pl.pallas_call(
        functools.partial(_merge_kernel, tm=tm),
        out_shape=(jax.ShapeDtypeStruct((n, d), F32),
                   jax.ShapeDtypeStruct((n, d // 2), jnp.uint32),
                   jax.ShapeDtypeStruct((n, LANES), jnp.int32),
                   jax.ShapeDtypeStruct((n, LANES), F32),
                   jax.ShapeDtypeStruct((8, LANES), jnp.int32)),
        grid=(n // tm,),
        in_specs=[pl.BlockSpec((tm, y_sb.shape[1]), row),
                  pl.BlockSpec((tm, y_ret.shape[1]), row),
                  pl.BlockSpec((tm, d), lambda i: (i, off_gsb // d)),
                  pl.BlockSpec((tm, d), lambda i: (i, off_gret // d)),
                  pl.BlockSpec((tm, d), row),
                  pl.BlockSpec(wsb.shape, const),
                  pl.BlockSpec(wret.shape, const),
                  pl.BlockSpec(wout.shape, const),
                  pl.BlockSpec((1, d), const),
                  pl.BlockSpec((1, d), const),
                  pl.BlockSpec(wr.shape, const),
                  pl.BlockSpec((1, LANES), const)],
        out_specs=(pl.BlockSpec((tm, d), row),
                   pl.BlockSpec((tm, d // 2), row),
                   pl.BlockSpec((tm, LANES), row),
                   pl.BlockSpec((tm, LANES), row),
                   pl.BlockSpec((8, LANES), const)),
        scratch_shapes=[pltpu.VMEM((1, LANES), F32)],
        compiler_params=_cparams(("arbitrary",)),
        name="merge_route",
    )(y_sb, y_ret, proj, proj, x2, wsb, wret, wout, g1, b1, wr, br)


def _dispatch_kernel(starts_ref, eidx_ref, rank_ref, x_ref, zeros_hbm, xs_hbm, sem, *, tm):
    del zeros_hbm

    def row_copy(t, dest):
        return pltpu.make_async_copy(x_ref.at[pl.ds(t, 1)], xs_hbm.at[pl.ds(dest, 1)], sem)

    def issue(t, c):
        for k in range(TOP_K):
            j = t * TOP_K + k
            row_copy(t, starts_ref[eidx_ref[0, 0, j]] + rank_ref[0, 0, j]).start()
        return c

    lax.fori_loop(0, tm, issue, 0)

    def drain(j, c):
        row_copy(0, 0).wait()
        return c

    lax.fori_loop(0, tm * TOP_K, drain, 0)


def _dispatch(starts, eidx, rank, x1p, n_rows, tm):
    n, w = x1p.shape
    nt = n // tm
    zeros = jnp.zeros((n_rows, w), x1p.dtype)
    smem_blk = pl.BlockSpec((1, 1, tm * TOP_K), lambda i, st: (i, 0, 0), memory_space=pltpu.SMEM)
    grid_spec = pltpu.PrefetchScalarGridSpec(
        num_scalar_prefetch=1,
        grid=(nt,),
        in_specs=[smem_blk, smem_blk,
                  pl.BlockSpec((tm, w), lambda i, st: (i, 0)),
                  pl.BlockSpec(memory_space=pl.ANY)],
        out_specs=pl.BlockSpec(memory_space=pl.ANY),
        scratch_shapes=[pltpu.SemaphoreType.DMA])
    return pl.pallas_call(
        functools.partial(_dispatch_kernel, tm=tm),
        out_shape=jax.ShapeDtypeStruct((n_rows, w), x1p.dtype),
        grid_spec=grid_spec,
        input_output_aliases={4: 0},
        compiler_params=_cparams(("arbitrary",)),
        name="dispatch",
    )(starts, eidx.reshape(nt, 1, tm * TOP_K), rank.reshape(nt, 1, tm * TOP_K), x1p, zeros)


def _expert_kernel(be_ref, nb_ref, xs_ref, wgu_ref, bgu_ref, wd_ref, bd_ref, o_ref):
    del be_ref
    i = pl.program_id(0)

    @pl.when(i < nb_ref[0])
    def _():
        words = xs_ref[...]
        left = lax.bitcast_convert_type(words & jnp.uint32(0xFFFF0000), F32)
        right = lax.bitcast_convert_type(words << 16, F32)
        xb = jnp.concatenate([left, right], axis=-1).astype(BF16)
        hgu = jnp.dot(xb, wgu_ref[...], preferred_element_type=F32) + bgu_ref[...]
        d_ff = hgu.shape[1] // 2
        gate = jnp.minimum(hgu[:, :d_ff], SWIGLU_LIMIT)
        up = jnp.clip(hgu[:, d_ff:], -SWIGLU_LIMIT, SWIGLU_LIMIT)
        glu = gate * jax.nn.sigmoid(gate * SWIGLU_ALPHA)
        act = ((up + 1.0) * glu).astype(BF16)
        o_ref[...] = jnp.dot(act, wd_ref[...], preferred_element_type=F32) + bd_ref[...]

    @pl.when(i >= nb_ref[0])
    def _():
        o_ref[...] = jnp.zeros_like(o_ref)


def _experts(block_expert, n_used, xs, wgu, bgu, wd, bd):
    n_rows, w = xs.shape
    d = wd.shape[2]
    n_blocks = n_rows // EXPERT_BLOCK
    grid_spec = pltpu.PrefetchScalarGridSpec(
        num_scalar_prefetch=2,
        grid=(n_blocks,),
        in_specs=[pl.BlockSpec((EXPERT_BLOCK, w), lambda i, be, nb: (i, 0)),
                  pl.BlockSpec((None,) + wgu.shape[1:], lambda i, be, nb: (be[i], 0, 0)),
                  pl.BlockSpec((None,) + bgu.shape[1:], lambda i, be, nb: (be[i], 0, 0)),
                  pl.BlockSpec((None,) + wd.shape[1:], lambda i, be, nb: (be[i], 0, 0)),
                  pl.BlockSpec((None,) + bd.shape[1:], lambda i, be, nb: (be[i], 0, 0))],
        out_specs=pl.BlockSpec((EXPERT_BLOCK, d), lambda i, be, nb: (i, 0)))
    return pl.pallas_call(
        _expert_kernel,
        out_shape=jax.ShapeDtypeStruct((n_rows, d), F32),
        grid_spec=grid_spec,
        compiler_params=_cparams(("arbitrary",)),
        name="experts",
    )(block_expert, n_used, xs, wgu, bgu, wd, bd)


def _combine_kernel(starts_ref, eidx_ref, rank_ref, x1_ref, topw_ref, g2_ref, b2_ref, rows_hbm,
                    o_ref, buf_ref, sem, *, tc):
    def row_copy(t, k, src):
        return pltpu.make_async_copy(rows_hbm.at[pl.ds(src, 1)], buf_ref.at[k, pl.ds(t, 1)], sem)

    def issue(t, c):
        for k in range(TOP_K):
            j = t * TOP_K + k
            row_copy(t, k, starts_ref[eidx_ref[0, 0, j]] + rank_ref[0, 0, j]).start()
        return c

    lax.fori_loop(0, tc, issue, 0)

    def drain(j, c):
        row_copy(0, 0, 0).wait()
        return c

    lax.fori_loop(0, tc * TOP_K, drain, 0)

    topw = topw_ref[...]
    y = topw[:, 0:1] * buf_ref[0]
    for k in range(1, TOP_K):
        y = y + topw[:, k:k + 1] * buf_ref[k]
    o_ref[...] = _layer_norm(DEEPNORM_ALPHA * x1_ref[...] + y, g2_ref[...], b2_ref[...])


def _combine(starts, eidx, rank, x1, topw, g2, b2, rows, tc):
    n, d = x1.shape
    nt = n // tc
    smem_blk = pl.BlockSpec((1, 1, tc * TOP_K), lambda i, st: (i, 0, 0), memory_space=pltpu.SMEM)
    grid_spec = pltpu.PrefetchScalarGridSpec(
        num_scalar_prefetch=1,
        grid=(nt,),
        in_specs=[smem_blk, smem_blk,
                  pl.BlockSpec((tc, d), lambda i, st: (i, 0)),
                  pl.BlockSpec((tc, LANES), lambda i, st: (i, 0)),
                  pl.BlockSpec((1, d), lambda i, st: (0, 0)),
                  pl.BlockSpec((1, d), lambda i, st: (0, 0)),
                  pl.BlockSpec(memory_space=pl.ANY)],
        out_specs=pl.BlockSpec((tc, d), lambda i, st: (i, 0)),
        scratch_shapes=[pltpu.VMEM((TOP_K, tc, d), F32), pltpu.SemaphoreType.DMA])
    return pl.pallas_call(
        functools.partial(_combine_kernel, tc=tc),
        out_shape=jax.ShapeDtypeStruct((n, d), F32),
        grid_spec=grid_spec,
        compiler_params=_cparams(("arbitrary",)),
        name="combine",
    )(starts, eidx.reshape(nt, 1, tc * TOP_K), rank.reshape(nt, 1, tc * TOP_K),
      x1, topw, g2, b2, rows)


def _tile(n, pref):
    t = min(n, pref)
    assert n % t == 0, (n, t)
    return t


def kernel(x, w_in, w_branch_sb, w_branch_ret, w_out, ln1_g, ln1_b, w_router, b_router,
           w_gate_up, b_gate_up, w_down, b_down, ln2_g, ln2_b):
    b, s, d = x.shape
    n = b * s
    sb_w = SB_HEADS * SB_HEAD_DIM
    qk_w = RET_HEADS * RET_QK_DIM
    v_w = RET_HEADS * RET_V_DIM
    sizes = (sb_w, sb_w, sb_w, qk_w, qk_w, v_w, v_w, d, d)
    offs = [int(o) for o in np.cumsum((0,) + sizes)]
    assert w_in.shape[0] == DEPTH and w_in.shape[2] == offs[-1]
    assert d == sb_w and d % (2 * LANES) == 0

    half = RET_QK_DIM // 2
    inv_freq = ROPE_BASE ** (-jnp.arange(half, dtype=F32) / half)
    ang = jnp.arange(s, dtype=F32)[:, None] * inv_freq[None, :]
    cos, sin = jnp.cos(ang), jnp.sin(ang)

    n_assign = n * TOP_K
    n_rows = -(-n_assign // EXPERT_BLOCK) * EXPERT_BLOCK + N_EXPERTS * EXPERT_BLOCK
    n_blocks = n_rows // EXPERT_BLOCK

    xcur = x.reshape(n, d)
    for l in range(DEPTH):
        proj = _in_proj(xcur, w_in[l].astype(BF16), _tile(n, 1024), _tile(offs[-1], 1024))
        proj3 = proj.reshape(b, s, offs[-1])
        y_sb = _sb_attention(proj3, _tile(s, 256))
        y_ret = _retention(proj3, cos, sin, _tile(s, 256), offs[3], offs[4], offs[5], offs[6])

        wr = jnp.zeros((d, LANES), F32).at[:, :N_EXPERTS].set(w_router[l])
        br = jnp.zeros((1, LANES), F32).at[0, :N_EXPERTS].set(b_router[l])
        x1, x1p, route, topw, cnt = _merge_route(
            y_sb.reshape(n, sb_w), y_ret.reshape(n, v_w), proj, xcur,
            w_branch_sb[l].astype(BF16), w_branch_ret[l].astype(BF16), w_out[l].astype(BF16),
            ln1_g[l].reshape(1, d), ln1_b[l].reshape(1, d), wr, br,
            _tile(n, 512), offs[7], offs[8])

        counts = cnt[0, :N_EXPERTS]
        padded = (counts + EXPERT_BLOCK - 1) // EXPERT_BLOCK * EXPERT_BLOCK
        padded_ends = jnp.cumsum(padded)
        starts = (padded_ends - padded).astype(jnp.int32)
        n_used = (padded_ends[-1:] // EXPERT_BLOCK).astype(jnp.int32)
        block_expert = jnp.minimum(
            jnp.searchsorted(padded_ends, jnp.arange(n_blocks) * EXPERT_BLOCK, side='right'),
            N_EXPERTS - 1).astype(jnp.int32)
        eidx = route[:, :TOP_K]
        rank = route[:, TOP_K:2 * TOP_K]

        xs = _dispatch(starts, eidx, rank, x1p, n_rows, _tile(n, 512))
        rows = _experts(block_expert, n_used, xs,
                        w_gate_up[l].astype(BF16), b_gate_up[l][:, None, :],
                        w_down[l].astype(BF16), b_down[l][:, None, :])
        xcur = _combine(starts, eidx, rank, x1, topw, ln2_g[l].reshape(1, d),
                        ln2_b[l].reshape(1, d), rows, _tile(n, 256))
    return xcur.reshape(b, s, d)
```

```python
import functools

import numpy as np
import jax
import jax.numpy as jnp
from jax import lax
from jax.experimental import pallas as pl
from jax.experimental.pallas import tpu as pltpu

F32 = jnp.float32
BF16 = jnp.bfloat16

SB_HEADS = 16
SB_HEAD_DIM = 64
RET_HEADS = 4
RET_QK_DIM = 256
RET_V_DIM = 512
RET_CHUNK = 64
ROPE_BASE = 10000.0
N_EXPERTS = 32
TOP_K = 4
EXPERT_BLOCK = 256
SWIGLU_LIMIT = 7.0
SWIGLU_ALPHA = 1.702
LN_EPS = 1e-5
DEPTH = 1
DEEPNORM_ALPHA = (2.0 * DEPTH) ** 0.25

LANES = 128
SUBLANES = 8
VMEM_LIMIT = 56 * 1024 * 1024

EXP_UNDERFLOW = -104.0


def _cparams(sem):
    return pltpu.CompilerParams(dimension_semantics=sem, vmem_limit_bytes=VMEM_LIMIT)


def _proj_kernel(x_ref, w_ref, o_ref, xb_ref):
    @pl.when(pl.program_id(1) == 0)
    def _():
        xb_ref[...] = x_ref[...].astype(BF16)

    o_ref[...] = jnp.dot(xb_ref[...], w_ref[...],
                         preferred_element_type=F32).astype(o_ref.dtype)


def _in_proj(x2, w_bf, tm, tn):
    n, d = x2.shape
    width = w_bf.shape[1]
    return pl.pallas_call(
        _proj_kernel,
        out_shape=jax.ShapeDtypeStruct((n, width), BF16),
        grid=(n // tm, width // tn),
        in_specs=[pl.BlockSpec((tm, d), lambda i, j: (i, 0)),
                  pl.BlockSpec((d, tn), lambda i, j: (0, j))],
        out_specs=pl.BlockSpec((tm, tn), lambda i, j: (i, j)),
        scratch_shapes=[pltpu.VMEM((tm, d), BF16)],
        compiler_params=_cparams(("parallel", "arbitrary")),
        name="in_proj",
    )(x2, w_bf)


def _sb_kernel(q_ref, k_ref, v_ref, o_ref, acc_ref, carry_ref, *, tq, n_heads):
    qi = pl.program_id(2)
    n_groups = n_heads // 2
    lane = lax.broadcasted_iota(jnp.int32, (tq, LANES), 1)
    low = lane < SB_HEAD_DIM
    q = q_ref[...] * jnp.asarray(SB_HEAD_DIM ** -0.5, BF16)
    zero = jnp.zeros((tq, LANES), BF16)
    q_pairs = []
    for g in range(n_groups):
        qg = q[:, g * LANES:(g + 1) * LANES]
        q_pairs.append(jnp.concatenate([jnp.where(low, qg, zero), jnp.where(low, zero, qg)], axis=0))
    row = lax.broadcasted_iota(jnp.int32, (tq, tq), 0)
    col = lax.broadcasted_iota(jnp.int32, (tq, tq), 1)
    suffix = (row >= col).astype(BF16)
    rows_all = lax.broadcasted_iota(jnp.int32, (n_heads * tq, tq), 0)
    cols_all = lax.broadcasted_iota(jnp.int32, (n_heads * tq, tq), 1)
    strict = cols_all < (rows_all & (tq - 1))

    acc_ref[...] = jnp.zeros_like(acc_ref)
    carry_ref[...] = jnp.zeros_like(carry_ref)

    def block(kb, diag):
        start = pl.multiple_of(kb * tq, tq)
        z = jnp.concatenate(
            [lax.dot_general(q_pairs[g], k_ref[pl.ds(start, tq), pl.ds(g * LANES, LANES)],
                             (((1,), (1,)), ((), ())), preferred_element_type=F32)
             for g in range(n_groups)], axis=0)
        lsm = jnp.minimum(-z, 0.0) - jnp.log(1.0 + jnp.exp(-jnp.abs(z)))
        if diag:
            lsm = jnp.where(strict, lsm, 0.0)
        hi = lsm.astype(BF16)
        lo = (lsm - hi.astype(F32)).astype(BF16)
        cum = (jnp.dot(hi, suffix, preferred_element_type=F32)
               + jnp.dot(lo, suffix, preferred_element_type=F32))
        carry = carry_ref[...]
        a = jnp.exp(z + cum + carry)
        if diag:
            a = jnp.where(strict, a, 0.0)
        a = a.astype(BF16)
        pv = jnp.concatenate(
            [jnp.dot(a[g * 2 * tq:(g + 1) * 2 * tq], v_ref[pl.ds(start, tq), pl.ds(g * LANES, LANES)],
                     preferred_element_type=F32)
             for g in range(n_groups)], axis=0)
        acc_ref[...] += pv
        carry_ref[...] = carry + cum[:, 0:1]

    def alive():
        return jnp.max(carry_ref[...]) >= EXP_UNDERFLOW

    block(qi, True)

    def cond(c):
        kb, live = c
        return jnp.logical_and(kb >= 0, live)

    def body(c):
        kb, _ = c
        block(kb, False)
        return kb - 1, alive()

    lax.while_loop(cond, body, (qi - 1, alive()))
    for g in range(n_groups):
        o_ref[:, g * LANES:(g + 1) * LANES] = jnp.where(
            low, acc_ref[pl.ds(2 * g * tq, tq), :], acc_ref[pl.ds((2 * g + 1) * tq, tq), :]
        ).astype(o_ref.dtype)


def _sb_attention(proj3, tq, n_heads):
    b, s, _ = proj3.shape
    width = n_heads * SB_HEAD_DIM
    groups = SB_HEADS // n_heads
    resident = pl.Buffered(1)
    return pl.pallas_call(
        functools.partial(_sb_kernel, tq=tq, n_heads=n_heads),
        out_shape=jax.ShapeDtypeStruct((b, s, SB_HEADS * SB_HEAD_DIM), BF16),
        grid=(b, groups, s // tq),
        in_specs=[pl.BlockSpec((None, tq, width), lambda bi, g, i: (bi, i, g)),
                  pl.BlockSpec((None, s, width), lambda bi, g, i: (bi, 0, groups + g),
                               pipeline_mode=resident),
                  pl.BlockSpec((None, s, width), lambda bi, g, i: (bi, 0, 2 * groups + g),
                               pipeline_mode=resident)],
        out_specs=pl.BlockSpec((None, tq, width), lambda bi, g, i: (bi, i, g)),
        scratch_shapes=[pltpu.VMEM((n_heads * tq, LANES), F32), pltpu.VMEM((n_heads * tq, 1), F32)],
        compiler_params=_cparams(("parallel", "parallel", "arbitrary")),
        name="sb_attn",
    )(proj3, proj3, proj3)


def _ret_kernel(lg_ref, q_ref, k_ref, v_ref, g_ref, cos_ref, sin_ref, o_ref,
                state_ref, mask_ref, qdec_ref, kdec_ref, *, tb):
    h = pl.program_id(1)
    si = pl.program_id(2)
    lg = lg_ref[h]
    half = RET_QK_DIM // 2

    @pl.when(si == 0)
    def _():
        state_ref[...] = jnp.zeros_like(state_ref)
        i = lax.broadcasted_iota(jnp.int32, (tb, tb), 0)
        j = lax.broadcasted_iota(jnp.int32, (tb, tb), 1)
        dist = jnp.abs(i - j).astype(F32)
        visible = (j // RET_CHUNK) <= (i // RET_CHUNK)
        mask_ref[...] = jnp.where(visible, jnp.exp(lg * dist), 0.0)
        r = lax.broadcasted_iota(jnp.int32, (tb, 1), 0).astype(F32)
        qdec_ref[...] = jnp.exp(lg * (r + 1.0))
        kdec_ref[...] = jnp.exp(lg * (tb - 1.0 - r))

    cos = cos_ref[...]
    sin = sin_ref[...]

    def rot(t):
        t1 = t[:, :half]
        t2 = t[:, half:]
        return jnp.concatenate([t1 * cos - t2 * sin, t1 * sin + t2 * cos], axis=-1)

    qr = rot(q_ref[...].astype(F32)) * (RET_QK_DIM ** -0.5)
    kr = rot(k_ref[...].astype(F32))
    v = v_ref[...]
    scores = lax.dot_general(qr.astype(BF16), kr.astype(BF16), (((1,), (1,)), ((), ())),
                             preferred_element_type=F32) * mask_ref[...]
    state = state_ref[...]
    y = (jnp.dot(scores.astype(BF16), v, preferred_element_type=F32)
         + jnp.dot((qr * qdec_ref[...]).astype(BF16), state.astype(BF16),
                   preferred_element_type=F32))
    kd = (kr * kdec_ref[...]).astype(BF16)
    state_ref[...] = state * jnp.exp(lg * tb) + lax.dot_general(
        kd, v, (((0,), (0,)), ((), ())), preferred_element_type=F32)

    mu = jnp.mean(y, axis=-1, keepdims=True)
    yc = y - mu
    var = jnp.mean(yc * yc, axis=-1, keepdims=True)
    yn = yc * lax.rsqrt(var + LN_EPS)
    g = g_ref[...].astype(F32)
    o_ref[...] = (g * jax.nn.sigmoid(g) * yn).astype(o_ref.dtype)


def _retention(proj3, cos, sin, tb, off_q, off_k, off_v, off_g):
    b, s, _ = proj3.shape
    log_gamma = jnp.asarray(np.log(1.0 - 2.0 ** (-5.0 - np.arange(RET_HEADS))), F32)
    qb, kb = off_q // RET_QK_DIM, off_k // RET_QK_DIM
    vb, gb = off_v // RET_V_DIM, off_g // RET_V_DIM
    half = RET_QK_DIM // 2
    grid_spec = pltpu.PrefetchScalarGridSpec(
        num_scalar_prefetch=1,
        grid=(b, RET_HEADS, s // tb),
        in_specs=[pl.BlockSpec((None, tb, RET_QK_DIM), lambda bi, h, i, lg: (bi, i, qb + h)),
                  pl.BlockSpec((None, tb, RET_QK_DIM), lambda bi, h, i, lg: (bi, i, kb + h)),
                  pl.BlockSpec((None, tb, RET_V_DIM), lambda bi, h, i, lg: (bi, i, vb + h)),
                  pl.BlockSpec((None, tb, RET_V_DIM), lambda bi, h, i, lg: (bi, i, gb + h)),
                  pl.BlockSpec((tb, half), lambda bi, h, i, lg: (i, 0)),
                  pl.BlockSpec((tb, half), lambda bi, h, i, lg: (i, 0))],
        out_specs=pl.BlockSpec((None, tb, RET_V_DIM), lambda bi, h, i, lg: (bi, i, h)),
        scratch_shapes=[pltpu.VMEM((RET_QK_DIM, RET_V_DIM), F32),
                        pltpu.VMEM((tb, tb), F32),
                        pltpu.VMEM((tb, 1), F32),
                        pltpu.VMEM((tb, 1), F32)])
    return pl.pallas_call(
        functools.partial(_ret_kernel, tb=tb),
        out_shape=jax.ShapeDtypeStruct((b, s, RET_HEADS * RET_V_DIM), BF16),
        grid_spec=grid_spec,
        compiler_params=_cparams(("parallel", "parallel", "arbitrary")),
        name="retention",
    )(log_gamma, proj3, proj3, proj3, proj3, cos, sin)


def _layer_norm(h, g, b):
    mu = jnp.mean(h, axis=-1, keepdims=True)
    hc = h - mu
    var = jnp.mean(hc * hc, axis=-1, keepdims=True)
    return hc * lax.rsqrt(var + LN_EPS) * g + b


def _merge_kernel(ysb_ref, yret_ref, gsb_ref, gret_ref, x_ref, wsb_ref, wret_ref, wout_ref,
                  g1_ref, b1_ref, wrh_ref, wrl_ref, br_ref,
                  x1_ref, route_ref, topw_ref, cnt_ref, run_ref, *, tm):
    i = pl.program_id(0)

    @pl.when(i == 0)
    def _():
        run_ref[...] = jnp.zeros_like(run_ref)

    bsb = jnp.dot(ysb_ref[...], wsb_ref[...], preferred_element_type=F32)
    bret = jnp.dot(yret_ref[...], wret_ref[...], preferred_element_type=F32)
    merged = (jax.nn.sigmoid(gsb_ref[...].astype(F32)) * bsb
              + jax.nn.sigmoid(gret_ref[...].astype(F32)) * bret)
    mix = jnp.dot(merged.astype(BF16), wout_ref[...], preferred_element_type=F32)
    x1 = _layer_norm(DEEPNORM_ALPHA * x_ref[...] + mix, g1_ref[...], b1_ref[...])
    x1_ref[...] = x1

    x_hi = x1.astype(BF16)
    x_lo = (x1 - x_hi.astype(F32)).astype(BF16)
    logits = (jnp.dot(x_hi, wrh_ref[...], preferred_element_type=F32)
              + jnp.dot(x_hi, wrl_ref[...], preferred_element_type=F32)
              + jnp.dot(x_lo, wrh_ref[...], preferred_element_type=F32)) + br_ref[...]
    lane = lax.broadcasted_iota(jnp.int32, (tm, LANES), 1)
    lane_f = lane.astype(F32)
    neg = jnp.asarray(-jnp.inf, F32)
    work = jnp.where(lane < N_EXPERTS, logits, neg)
    vals, idxs = [], []
    sel = jnp.zeros((tm, LANES), F32)
    for _ in range(TOP_K):
        m = jnp.max(work, axis=-1, keepdims=True)
        idx = jnp.min(jnp.where(work == m, lane_f, float(LANES)), axis=-1, keepdims=True)
        hit = lane_f == idx
        sel = jnp.where(hit, 1.0, sel)
        work = jnp.where(hit, neg, work)
        vals.append(m)
        idxs.append(idx)
    exps = [jnp.exp(v - vals[0]) for v in vals]
    denom = exps[0] + exps[1] + exps[2] + exps[3]
    weights = [e / denom for e in exps]

    r = lax.broadcasted_iota(jnp.int32, (tm, tm), 0)
    c = lax.broadcasted_iota(jnp.int32, (tm, tm), 1)
    before = (c < r).astype(BF16)
    rank_mat = jnp.dot(before, sel.astype(BF16), preferred_element_type=F32) + run_ref[...]
    route = jnp.zeros((tm, LANES), F32)
    topw = jnp.zeros((tm, LANES), F32)
    for k in range(TOP_K):
        rank_k = jnp.sum(jnp.where(lane_f == idxs[k], rank_mat, 0.0), axis=-1, keepdims=True)
        route = jnp.where(lane == k, idxs[k], route)
        route = jnp.where(lane == TOP_K + k, rank_k, route)
        topw = jnp.where(lane == k, weights[k], topw)
    route_ref[...] = route.T[:2 * TOP_K, :].astype(jnp.int32)
    topw_ref[...] = topw
    run = run_ref[...] + jnp.sum(sel, axis=0, keepdims=True)
    run_ref[...] = run
    cnt_ref[...] = jnp.broadcast_to(run, cnt_ref.shape).astype(jnp.int32)


def _merge_route(y_sb, y_ret, proj, x2, wsb, wret, wout, g1, b1, wr_hi, wr_lo, br, tm,
                 off_gsb, off_gret):
    n, d = x2.shape
    const = lambda i: (0, 0)
    row = lambda i: (i, 0)
    return pl.pallas_call(
        functools.partial(_merge_kernel, tm=tm),
        out_shape=(jax.ShapeDtypeStruct((n, d), F32),
                   jax.ShapeDtypeStruct((2 * TOP_K, n), jnp.int32),
                   jax.ShapeDtypeStruct((n, LANES), F32),
                   jax.ShapeDtypeStruct((8, LANES), jnp.int32)),
        grid=(n // tm,),
        in_specs=[pl.BlockSpec((tm, y_sb.shape[1]), row),
                  pl.BlockSpec((tm, y_ret.shape[1]), row),
                  pl.BlockSpec((tm, d), lambda i: (i, off_gsb // d)),
                  pl.BlockSpec((tm, d), lambda i: (i, off_gret // d)),
                  pl.BlockSpec((tm, d), row),
                  pl.BlockSpec(wsb.shape, const),
                  pl.BlockSpec(wret.shape, const),
                  pl.BlockSpec(wout.shape, const),
                  pl.BlockSpec((1, d), const),
                  pl.BlockSpec((1, d), const),
                  pl.BlockSpec(wr_hi.shape, const),
                  pl.BlockSpec(wr_lo.shape, const),
                  pl.BlockSpec((1, LANES), const)],
        out_specs=(pl.BlockSpec((tm, d), row),
                   pl.BlockSpec((2 * TOP_K, tm), lambda i: (0, i)),
                   pl.BlockSpec((tm, LANES), row),
                   pl.BlockSpec((8, LANES), const)),
        scratch_shapes=[pltpu.VMEM((1, LANES), F32)],
        compiler_params=_cparams(("arbitrary",)),
        name="merge_route",
    )(y_sb, y_ret, proj, proj, x2, wsb, wret, wout, g1, b1, wr_hi, wr_lo, br)


PAD_CHUNKS = tuple(1 << b for b in reversed(range(SUBLANES.bit_length() - 1,
                                                  EXPERT_BLOCK.bit_length() - 1)))


def _slot(starts_ref, route_ref, k, t):
    return starts_ref[route_ref[k, t]] + route_ref[TOP_K + k, t]


def _dispatch_kernel(starts_ref, padstart_ref, padlen_ref, route_ref, x_ref, xs_hbm,
                     zero_ref, sem, zsem, *, tm):
    def pad_copies(act):
        def zero_copy(off, size):
            return pltpu.make_async_copy(zero_ref.at[pl.ds(0, size)],
                                         xs_hbm.at[pl.ds(off, size)], zsem)

        def per_expert(e, c):
            n = padlen_ref[e]
            off = padstart_ref[e]
            for r in range(SUBLANES - 1):
                @pl.when(r < (n & (SUBLANES - 1)))
                def _():
                    act(zero_copy(off + r, 1))
            off = pl.multiple_of(off + (n & (SUBLANES - 1)), SUBLANES)
            for size in PAD_CHUNKS:
                @pl.when((n & size) != 0)
                def _():
                    act(zero_copy(off, size))
                off = pl.multiple_of(off + (n & size), SUBLANES)
            return c

        lax.fori_loop(0, N_EXPERTS, per_expert, 0)

        def tail_chunk(j, c):
            off = pl.multiple_of(padstart_ref[N_EXPERTS] + j * PAD_CHUNKS[0], PAD_CHUNKS[0])
            act(zero_copy(off, PAD_CHUNKS[0]))
            return c

        lax.fori_loop(0, padlen_ref[N_EXPERTS] // PAD_CHUNKS[0], tail_chunk, 0)

    @pl.when(pl.program_id(0) == 0)
    def _():
        zero_ref[...] = jnp.zeros_like(zero_ref)
        pad_copies(lambda cp: cp.start())
        pad_copies(lambda cp: cp.wait())

    def issue(t, c):
        for k in range(TOP_K):
            pltpu.make_async_copy(x_ref.at[pl.ds(t, 1)],
                                  xs_hbm.at[pl.ds(_slot(starts_ref, route_ref, k, t), 1)], sem).start()
        return c

    lax.fori_loop(0, tm, issue, 0)
    for _ in range(TOP_K):
        pltpu.make_async_copy(x_ref, xs_hbm.at[pl.ds(0, tm)], sem).wait()


def _dispatch(starts, pad_start, pad_len, route, x1, n_rows, tm):
    n, d = x1.shape
    grid_spec = pltpu.PrefetchScalarGridSpec(
        num_scalar_prefetch=3,
        grid=(n // tm,),
        in_specs=[pl.BlockSpec((2 * TOP_K, tm), lambda i, *_: (0, i), memory_space=pltpu.SMEM),
                  pl.BlockSpec((tm, d), lambda i, *_: (i, 0))],
        out_specs=pl.BlockSpec(memory_space=pl.ANY),
        scratch_shapes=[pltpu.VMEM((PAD_CHUNKS[0], d), x1.dtype),
                        pltpu.SemaphoreType.DMA, pltpu.SemaphoreType.DMA])
    return pl.pallas_call(
        functools.partial(_dispatch_kernel, tm=tm),
        out_shape=jax.ShapeDtypeStruct((n_rows, d), x1.dtype),
        grid_spec=grid_spec,
        compiler_params=_cparams(("arbitrary",)),
        name="dispatch",
    )(starts, pad_start, pad_len, route, x1)


def _expert_kernel(be_ref, nb_ref, xs_ref, wgu_ref, bgu_ref, wd_ref, bd_ref, o_ref):
    del be_ref
    i = pl.program_id(0)

    @pl.when(i < nb_ref[0])
    def _():
        hgu = jnp.dot(xs_ref[...].astype(BF16), wgu_ref[...],
                      preferred_element_type=F32) + bgu_ref[...]
        d_ff = hgu.shape[1] // 2
        gate = jnp.minimum(hgu[:, :d_ff], SWIGLU_LIMIT)
        up = jnp.clip(hgu[:, d_ff:], -SWIGLU_LIMIT, SWIGLU_LIMIT)
        glu = gate * jax.nn.sigmoid(gate * SWIGLU_ALPHA)
        act = ((up + 1.0) * glu).astype(BF16)
        o_ref[...] = jnp.dot(act, wd_ref[...], preferred_element_type=F32) + bd_ref[...]

    @pl.when(i >= nb_ref[0])
    def _():
        o_ref[...] = jnp.zeros_like(o_ref)


def _experts(block_expert, n_used, xs, wgu, bgu, wd, bd):
    n_rows, w = xs.shape
    d = wd.shape[2]
    n_blocks = n_rows // EXPERT_BLOCK
    grid_spec = pltpu.PrefetchScalarGridSpec(
        num_scalar_prefetch=2,
        grid=(n_blocks,),
        in_specs=[pl.BlockSpec((EXPERT_BLOCK, w), lambda i, be, nb: (jnp.minimum(i, nb[0] - 1), 0)),
                  pl.BlockSpec((None,) + wgu.shape[1:], lambda i, be, nb: (be[i], 0, 0)),
                  pl.BlockSpec((None,) + bgu.shape[1:], lambda i, be, nb: (be[i], 0, 0)),
                  pl.BlockSpec((None,) + wd.shape[1:], lambda i, be, nb: (be[i], 0, 0)),
                  pl.BlockSpec((None,) + bd.shape[1:], lambda i, be, nb: (be[i], 0, 0))],
        out_specs=pl.BlockSpec((EXPERT_BLOCK, d), lambda i, be, nb: (i, 0)))
    return pl.pallas_call(
        _expert_kernel,
        out_shape=jax.ShapeDtypeStruct((n_rows, d), F32),
        grid_spec=grid_spec,
        compiler_params=_cparams(("arbitrary",)),
        name="experts",
    )(block_expert, n_used, xs, wgu, bgu, wd, bd)


def _combine_kernel(starts_ref, route_ref, route_next_ref,
                    x1_ref, topw_ref, g2_ref, b2_ref, rows_hbm,
                    o_ref, buf_ref, sem, *, tc):
    i = pl.program_id(0)
    slot = i % 2

    def issue_tile(r_ref, sl):
        def issue(t, c):
            for k in range(TOP_K):
                pltpu.make_async_copy(rows_hbm.at[pl.ds(_slot(starts_ref, r_ref, k, t), 1)],
                                      buf_ref.at[sl, k, pl.ds(t, 1)], sem.at[sl]).start()
            return c

        lax.fori_loop(0, tc, issue, 0)

    @pl.when(i == 0)
    def _():
        issue_tile(route_ref, 0)

    @pl.when(i + 1 < pl.num_programs(0))
    def _():
        issue_tile(route_next_ref, 1 - slot)

    for k in range(TOP_K):
        pltpu.make_async_copy(rows_hbm.at[pl.ds(0, tc)], buf_ref.at[slot, k], sem.at[slot]).wait()

    topw = topw_ref[...]
    y = topw[:, 0:1] * buf_ref[slot, 0]
    for k in range(1, TOP_K):
        y = y + topw[:, k:k + 1] * buf_ref[slot, k]
    o_ref[...] = _layer_norm(DEEPNORM_ALPHA * x1_ref[...] + y, g2_ref[...], b2_ref[...])


def _combine(starts, route, x1, topw, g2, b2, rows, tc):
    n, d = x1.shape
    nt = n // tc
    smem_blk = pl.BlockSpec((2 * TOP_K, tc), lambda i, st: (0, i), memory_space=pltpu.SMEM)
    smem_next = pl.BlockSpec((2 * TOP_K, tc), lambda i, st: (0, jnp.minimum(i + 1, nt - 1)),
                             memory_space=pltpu.SMEM)
    grid_spec = pltpu.PrefetchScalarGridSpec(
        num_scalar_prefetch=1,
        grid=(nt,),
        in_specs=[smem_blk, smem_next,
                  pl.BlockSpec((tc, d), lambda i, st: (i, 0)),
                  pl.BlockSpec((tc, LANES), lambda i, st: (i, 0)),
                  pl.BlockSpec((1, d), lambda i, st: (0, 0)),
                  pl.BlockSpec((1, d), lambda i, st: (0, 0)),
                  pl.BlockSpec(memory_space=pl.ANY)],
        out_specs=pl.BlockSpec((tc, d), lambda i, st: (i, 0)),
        scratch_shapes=[pltpu.VMEM((2, TOP_K, tc, d), F32), pltpu.SemaphoreType.DMA((2,))])
    return pl.pallas_call(
        functools.partial(_combine_kernel, tc=tc),
        out_shape=jax.ShapeDtypeStruct((n, d), F32),
        grid_spec=grid_spec,
        compiler_params=_cparams(("arbitrary",)),
        name="combine",
    )(starts, route, route, x1, topw, g2, b2, rows)


def _tile(n, pref):
    t = min(n, pref)
    assert n % t == 0, (n, t)
    return t


def kernel(x, w_in, w_branch_sb, w_branch_ret, w_out, ln1_g, ln1_b, w_router, b_router,
           w_gate_up, b_gate_up, w_down, b_down, ln2_g, ln2_b):
    b, s, d = x.shape
    n = b * s
    sb_w = SB_HEADS * SB_HEAD_DIM
    qk_w = RET_HEADS * RET_QK_DIM
    v_w = RET_HEADS * RET_V_DIM
    sizes = (sb_w, sb_w, sb_w, qk_w, qk_w, v_w, v_w, d, d)
    offs = [int(o) for o in np.cumsum((0,) + sizes)]
    assert w_in.shape[0] == DEPTH and w_in.shape[2] == offs[-1]
    assert d == sb_w and d % (2 * LANES) == 0

    half = RET_QK_DIM // 2
    inv_freq = ROPE_BASE ** (-jnp.arange(half, dtype=F32) / half)
    ang = jnp.arange(s, dtype=F32)[:, None] * inv_freq[None, :]
    cos, sin = jnp.cos(ang), jnp.sin(ang)

    n_assign = n * TOP_K
    n_rows = -(-n_assign // EXPERT_BLOCK) * EXPERT_BLOCK + N_EXPERTS * EXPERT_BLOCK
    n_blocks = n_rows // EXPERT_BLOCK

    xcur = x.reshape(n, d)
    for l in range(DEPTH):
        proj = _in_proj(xcur, w_in[l].astype(BF16), _tile(n, 1024), _tile(offs[-1], 1024))
        proj3 = proj.reshape(b, s, offs[-1])
        y_sb = _sb_attention(proj3, _tile(s, 128), 8)
        y_ret = _retention(proj3, cos, sin, _tile(s, 512), offs[3], offs[4], offs[5], offs[6])

        wr = jnp.zeros((d, LANES), F32).at[:, :N_EXPERTS].set(w_router[l])
        br = jnp.zeros((1, LANES), F32).at[0, :N_EXPERTS].set(b_router[l])
        wr_hi = wr.astype(BF16)
        wr_lo = (wr - wr_hi.astype(F32)).astype(BF16)
        x1, route, topw, cnt = _merge_route(
            y_sb.reshape(n, sb_w), y_ret.reshape(n, v_w), proj, xcur,
            w_branch_sb[l].astype(BF16), w_branch_ret[l].astype(BF16), w_out[l].astype(BF16),
            ln1_g[l].reshape(1, d), ln1_b[l].reshape(1, d), wr_hi, wr_lo, br,
            _tile(n, 512), offs[7], offs[8])

        counts = cnt[0, :N_EXPERTS]
        padded = (counts + EXPERT_BLOCK - 1) // EXPERT_BLOCK * EXPERT_BLOCK
        padded_ends = jnp.cumsum(padded)
        starts = (padded_ends - padded).astype(jnp.int32)
        n_used = (padded_ends[-1:] // EXPERT_BLOCK).astype(jnp.int32)
        block_rows = jnp.arange(n_blocks, dtype=jnp.int32) * EXPERT_BLOCK
        block_expert = jnp.minimum(
            jnp.sum((padded_ends[None, :] <= block_rows[:, None]).astype(jnp.int32), axis=1),
            N_EXPERTS - 1).astype(jnp.int32)
        pad_start = jnp.concatenate([starts + counts, padded_ends[-1:]]).astype(jnp.int32)
        pad_len = jnp.concatenate([padded - counts, n_rows - padded_ends[-1:]]).astype(jnp.int32)

        xs = _dispatch(starts, pad_start, pad_len, route, x1, n_rows, _tile(n, 512))
        rows = _experts(block_expert, n_used, xs,
                        w_gate_up[l].astype(BF16), b_gate_up[l][:, None, :],
                        w_down[l].astype(BF16), b_down[l][:, None, :])
        xcur = _combine(starts, route, x1, topw, ln2_g[l].reshape(1, d),
                        ln2_b[l].reshape(1, d), rows, _tile(n, 256))
    return xcur.reshape(b, s, d)
```

```python
import functools

import numpy as np
import jax
import jax.numpy as jnp
from jax import lax
from jax.experimental import pallas as pl
from jax.experimental.pallas import tpu as pltpu

F32 = jnp.float32
BF16 = jnp.bfloat16

SB_HEADS = 16
SB_HEAD_DIM = 64
RET_HEADS = 4
RET_QK_DIM = 256
RET_V_DIM = 512
RET_CHUNK = 64
ROPE_BASE = 10000.0
N_EXPERTS = 32
TOP_K = 4
EXPERT_BLOCK = 256
SWIGLU_LIMIT = 7.0
SWIGLU_ALPHA = 1.702
LN_EPS = 1e-5
DEPTH = 1
DEEPNORM_ALPHA = (2.0 * DEPTH) ** 0.25

LANES = 128
SUBLANES = 8
VMEM_LIMIT = 56 * 1024 * 1024

EXP2_UNDERFLOW_BITS = 151.0
LOG2_E = 1.4426950408889634


def _cparams(sem):
    return pltpu.CompilerParams(dimension_semantics=sem, vmem_limit_bytes=VMEM_LIMIT)


def _proj_kernel(x_ref, w_ref, cs_ref, o_ref, xb_ref):
    @pl.when(pl.program_id(1) == 0)
    def _():
        xb_ref[...] = x_ref[...].astype(BF16)

    acc = jnp.dot(xb_ref[...], w_ref[...], preferred_element_type=F32)
    o_ref[...] = (acc * cs_ref[...]).astype(o_ref.dtype)


def _in_proj(x2, w_bf, col_scale, tm, tn):
    n, d = x2.shape
    width = w_bf.shape[1]
    return pl.pallas_call(
        _proj_kernel,
        out_shape=jax.ShapeDtypeStruct((n, width), BF16),
        grid=(n // tm, width // tn),
        in_specs=[pl.BlockSpec((tm, d), lambda i, j: (i, 0)),
                  pl.BlockSpec((d, tn), lambda i, j: (0, j)),
                  pl.BlockSpec((1, tn), lambda i, j: (0, j))],
        out_specs=pl.BlockSpec((tm, tn), lambda i, j: (i, j)),
        scratch_shapes=[pltpu.VMEM((tm, d), BF16)],
        compiler_params=_cparams(("parallel", "arbitrary")),
        name="in_proj",
    )(x2, w_bf, col_scale)


def _sb_kernel(q_ref, k_ref, v_ref, o_ref, acc_ref, carry_ref, *, tq, n_heads):
    qi = pl.program_id(2)
    n_groups = n_heads // 2
    lane = lax.broadcasted_iota(jnp.int32, (tq, LANES), 1)
    low = lane < SB_HEAD_DIM
    q = q_ref[...]
    zero = jnp.zeros((tq, LANES), BF16)
    q_pairs = []
    for g in range(n_groups):
        qg = q[:, g * LANES:(g + 1) * LANES]
        q_pairs.append(jnp.concatenate([jnp.where(low, qg, zero), jnp.where(low, zero, qg)], axis=0))
    row = lax.broadcasted_iota(jnp.int32, (tq, tq), 0)
    col = lax.broadcasted_iota(jnp.int32, (tq, tq), 1)
    suffix = (row >= col).astype(BF16)
    rows_all = lax.broadcasted_iota(jnp.int32, (n_heads * tq, tq), 0)
    cols_all = lax.broadcasted_iota(jnp.int32, (n_heads * tq, tq), 1)
    strict = cols_all < (rows_all & (tq - 1))

    acc_ref[...] = jnp.zeros_like(acc_ref)
    carry_ref[...] = jnp.zeros_like(carry_ref)

    def block(kb, diag):
        start = pl.multiple_of(kb * tq, tq)
        z = jnp.concatenate(
            [lax.dot_general(q_pairs[g], k_ref[pl.ds(start, tq), pl.ds(g * LANES, LANES)],
                             (((1,), (1,)), ((), ())), preferred_element_type=F32)
             for g in range(n_groups)], axis=0)
        sp = jnp.maximum(z, 0.0) + jnp.log2(1.0 + jnp.exp2(-jnp.abs(z)))
        if diag:
            sp = jnp.where(strict, sp, 0.0)
        cum = jnp.dot(sp.astype(BF16), suffix, preferred_element_type=F32)
        carry = carry_ref[...]
        a = jnp.exp2(z - cum - carry)
        if diag:
            a = jnp.where(strict, a, 0.0)
        a = a.astype(BF16)
        pv = jnp.concatenate(
            [jnp.dot(a[g * 2 * tq:(g + 1) * 2 * tq], v_ref[pl.ds(start, tq), pl.ds(g * LANES, LANES)],
                     preferred_element_type=F32)
             for g in range(n_groups)], axis=0)
        acc_ref[...] += pv
        carry_ref[...] = carry + cum[:, 0:1]

    def alive():
        return jnp.min(carry_ref[...]) <= EXP2_UNDERFLOW_BITS

    block(qi, True)

    def cond(c):
        kb, live = c
        return jnp.logical_and(kb >= 0, live)

    def body(c):
        kb, _ = c
        block(kb, False)
        return kb - 1, alive()

    lax.while_loop(cond, body, (qi - 1, alive()))
    for g in range(n_groups):
        o_ref[:, g * LANES:(g + 1) * LANES] = jnp.where(
            low, acc_ref[pl.ds(2 * g * tq, tq), :], acc_ref[pl.ds((2 * g + 1) * tq, tq), :]
        ).astype(o_ref.dtype)


def _sb_attention(proj3, tq, n_heads):
    b, s, _ = proj3.shape
    width = n_heads * SB_HEAD_DIM
    groups = SB_HEADS // n_heads
    resident = pl.Buffered(1)
    return pl.pallas_call(
        functools.partial(_sb_kernel, tq=tq, n_heads=n_heads),
        out_shape=jax.ShapeDtypeStruct((b, s, SB_HEADS * SB_HEAD_DIM), BF16),
        grid=(b, groups, s // tq),
        in_specs=[pl.BlockSpec((None, tq, width), lambda bi, g, i: (bi, i, g)),
                  pl.BlockSpec((None, s, width), lambda bi, g, i: (bi, 0, groups + g),
                               pipeline_mode=resident),
                  pl.BlockSpec((None, s, width), lambda bi, g, i: (bi, 0, 2 * groups + g),
                               pipeline_mode=resident)],
        out_specs=pl.BlockSpec((None, tq, width), lambda bi, g, i: (bi, i, g)),
        scratch_shapes=[pltpu.VMEM((n_heads * tq, LANES), F32), pltpu.VMEM((n_heads * tq, 1), F32)],
        compiler_params=_cparams(("parallel", "parallel", "arbitrary")),
        name="sb_attn",
    )(proj3, proj3, proj3)


def _ret_kernel(lg_ref, q_ref, k_ref, v_ref, g_ref, cos_ref, sin_ref, o_ref,
                state_ref, mask_ref, qdec_ref, kdec_ref, *, tb):
    h = pl.program_id(1)
    si = pl.program_id(2)
    lg = lg_ref[h]
    half = RET_QK_DIM // 2

    @pl.when(si == 0)
    def _():
        state_ref[...] = jnp.zeros_like(state_ref)
        i = lax.broadcasted_iota(jnp.int32, (tb, tb), 0)
        j = lax.broadcasted_iota(jnp.int32, (tb, tb), 1)
        dist = jnp.abs(i - j).astype(F32)
        visible = (j // RET_CHUNK) <= (i // RET_CHUNK)
        mask_ref[...] = jnp.where(visible, jnp.exp(lg * dist), 0.0)
        r = lax.broadcasted_iota(jnp.int32, (tb, 1), 0).astype(F32)
        qdec_ref[...] = jnp.exp(lg * (r + 1.0))
        kdec_ref[...] = jnp.exp(lg * (tb - 1.0 - r))

    cos = cos_ref[...]
    sin = sin_ref[...]

    def rot(t):
        t1 = t[:, :half]
        t2 = t[:, half:]
        return jnp.concatenate([t1 * cos - t2 * sin, t1 * sin + t2 * cos], axis=-1)

    qr = rot(q_ref[...].astype(F32)) * (RET_QK_DIM ** -0.5)
    kr = rot(k_ref[...].astype(F32))
    v = v_ref[...]
    scores = lax.dot_general(qr.astype(BF16), kr.astype(BF16), (((1,), (1,)), ((), ())),
                             preferred_element_type=F32) * mask_ref[...]
    state = state_ref[...]
    y = (jnp.dot(scores.astype(BF16), v, preferred_element_type=F32)
         + jnp.dot((qr * qdec_ref[...]).astype(BF16), state.astype(BF16),
                   preferred_element_type=F32))
    kd = (kr * kdec_ref[...]).astype(BF16)
    state_ref[...] = state * jnp.exp(lg * tb) + lax.dot_general(
        kd, v, (((0,), (0,)), ((), ())), preferred_element_type=F32)

    mu = jnp.mean(y, axis=-1, keepdims=True)
    yc = y - mu
    var = jnp.mean(yc * yc, axis=-1, keepdims=True)
    yn = yc * lax.rsqrt(var + LN_EPS)
    g = g_ref[...].astype(F32)
    o_ref[...] = (g * jax.nn.sigmoid(g) * yn).astype(o_ref.dtype)


def _retention(proj3, cos, sin, tb, off_q, off_k, off_v, off_g):
    b, s, _ = proj3.shape
    log_gamma = jnp.asarray(np.log(1.0 - 2.0 ** (-5.0 - np.arange(RET_HEADS))), F32)
    qb, kb = off_q // RET_QK_DIM, off_k // RET_QK_DIM
    vb, gb = off_v // RET_V_DIM, off_g // RET_V_DIM
    half = RET_QK_DIM // 2
    grid_spec = pltpu.PrefetchScalarGridSpec(
        num_scalar_prefetch=1,
        grid=(b, RET_HEADS, s // tb),
        in_specs=[pl.BlockSpec((None, tb, RET_QK_DIM), lambda bi, h, i, lg: (bi, i, qb + h)),
                  pl.BlockSpec((None, tb, RET_QK_DIM), lambda bi, h, i, lg: (bi, i, kb + h)),
                  pl.BlockSpec((None, tb, RET_V_DIM), lambda bi, h, i, lg: (bi, i, vb + h)),
                  pl.BlockSpec((None, tb, RET_V_DIM), lambda bi, h, i, lg: (bi, i, gb + h)),
                  pl.BlockSpec((tb, half), lambda bi, h, i, lg: (i, 0)),
                  pl.BlockSpec((tb, half), lambda bi, h, i, lg: (i, 0))],
        out_specs=pl.BlockSpec((None, tb, RET_V_DIM), lambda bi, h, i, lg: (bi, i, h)),
        scratch_shapes=[pltpu.VMEM((RET_QK_DIM, RET_V_DIM), F32),
                        pltpu.VMEM((tb, tb), F32),
                        pltpu.VMEM((tb, 1), F32),
                        pltpu.VMEM((tb, 1), F32)])
    return pl.pallas_call(
        functools.partial(_ret_kernel, tb=tb),
        out_shape=jax.ShapeDtypeStruct((b, s, RET_HEADS * RET_V_DIM), BF16),
        grid_spec=grid_spec,
        compiler_params=_cparams(("parallel", "parallel", "arbitrary")),
        name="retention",
    )(log_gamma, proj3, proj3, proj3, proj3, cos, sin)


def _layer_norm(h, g, b):
    mu = jnp.mean(h, axis=-1, keepdims=True)
    hc = h - mu
    var = jnp.mean(hc * hc, axis=-1, keepdims=True)
    return hc * lax.rsqrt(var + LN_EPS) * g + b


def _merge_kernel(ysb_ref, yret_ref, gsb_ref, gret_ref, x_ref, wsb_ref, wret_ref, wout_ref,
                  g1_ref, b1_ref, wrh_ref, wrl_ref, br_ref,
                  x1_ref, route_ref, topw_ref, cnt_ref, run_ref, *, tm):
    i = pl.program_id(0)

    @pl.when(i == 0)
    def _():
        run_ref[...] = jnp.zeros_like(run_ref)

    bsb = jnp.dot(ysb_ref[...], wsb_ref[...], preferred_element_type=F32)
    bret = jnp.dot(yret_ref[...], wret_ref[...], preferred_element_type=F32)
    merged = (jax.nn.sigmoid(gsb_ref[...].astype(F32)) * bsb
              + jax.nn.sigmoid(gret_ref[...].astype(F32)) * bret)
    mix = jnp.dot(merged.astype(BF16), wout_ref[...], preferred_element_type=F32)
    x1 = _layer_norm(DEEPNORM_ALPHA * x_ref[...] + mix, g1_ref[...], b1_ref[...])
    x1_ref[...] = x1

    x_hi = x1.astype(BF16)
    x_lo = (x1 - x_hi.astype(F32)).astype(BF16)
    logits = (jnp.dot(x_hi, wrh_ref[...], preferred_element_type=F32)
              + jnp.dot(x_hi, wrl_ref[...], preferred_element_type=F32)
              + jnp.dot(x_lo, wrh_ref[...], preferred_element_type=F32)) + br_ref[...]
    lane = lax.broadcasted_iota(jnp.int32, (tm, LANES), 1)
    lane_f = lane.astype(F32)
    neg = jnp.asarray(-jnp.inf, F32)
    work = jnp.where(lane < N_EXPERTS, logits, neg)
    vals, idxs = [], []
    sel = jnp.zeros((tm, LANES), F32)
    for _ in range(TOP_K):
        m = jnp.max(work, axis=-1, keepdims=True)
        idx = jnp.min(jnp.where(work == m, lane_f, float(LANES)), axis=-1, keepdims=True)
        hit = lane_f == idx
        sel = jnp.where(hit, 1.0, sel)
        work = jnp.where(hit, neg, work)
        vals.append(m)
        idxs.append(idx)
    exps = [jnp.exp(v - vals[0]) for v in vals]
    denom = exps[0] + exps[1] + exps[2] + exps[3]
    weights = [e / denom for e in exps]

    r = lax.broadcasted_iota(jnp.int32, (tm, tm), 0)
    c = lax.broadcasted_iota(jnp.int32, (tm, tm), 1)
    before = (c < r).astype(BF16)
    rank_mat = jnp.dot(before, sel.astype(BF16), preferred_element_type=F32) + run_ref[...]
    route = jnp.zeros((tm, LANES), F32)
    topw = jnp.zeros((tm, LANES), F32)
    for k in range(TOP_K):
        rank_k = jnp.sum(jnp.where(lane_f == idxs[k], rank_mat, 0.0), axis=-1, keepdims=True)
        route = jnp.where(lane == k, idxs[k], route)
        route = jnp.where(lane == TOP_K + k, rank_k, route)
        topw = jnp.where(lane == k, weights[k], topw)
    route_ref[...] = route.T[:2 * TOP_K, :].astype(jnp.int32)
    topw_ref[...] = topw
    run = run_ref[...] + jnp.sum(sel, axis=0, keepdims=True)
    run_ref[...] = run
    cnt_ref[...] = jnp.broadcast_to(run, cnt_ref.shape).astype(jnp.int32)


def _merge_route(y_sb, y_ret, proj, x2, wsb, wret, wout, g1, b1, wr_hi, wr_lo, br, tm,
                 off_gsb, off_gret):
    n, d = x2.shape
    const = lambda i: (0, 0)
    row = lambda i: (i, 0)
    return pl.pallas_call(
        functools.partial(_merge_kernel, tm=tm),
        out_shape=(jax.ShapeDtypeStruct((n, d), F32),
                   jax.ShapeDtypeStruct((2 * TOP_K, n), jnp.int32),
                   jax.ShapeDtypeStruct((n, LANES), F32),
                   jax.ShapeDtypeStruct((8, LANES), jnp.int32)),
        grid=(n // tm,),
        in_specs=[pl.BlockSpec((tm, y_sb.shape[1]), row),
                  pl.BlockSpec((tm, y_ret.shape[1]), row),
                  pl.BlockSpec((tm, d), lambda i: (i, off_gsb // d)),
                  pl.BlockSpec((tm, d), lambda i: (i, off_gret // d)),
                  pl.BlockSpec((tm, d), row),
                  pl.BlockSpec(wsb.shape, const),
                  pl.BlockSpec(wret.shape, const),
                  pl.BlockSpec(wout.shape, const),
                  pl.BlockSpec((1, d), const),
                  pl.BlockSpec((1, d), const),
                  pl.BlockSpec(wr_hi.shape, const),
                  pl.BlockSpec(wr_lo.shape, const),
                  pl.BlockSpec((1, LANES), const)],
        out_specs=(pl.BlockSpec((tm, d), row),
                   pl.BlockSpec((2 * TOP_K, tm), lambda i: (0, i)),
                   pl.BlockSpec((tm, LANES), row),
                   pl.BlockSpec((8, LANES), const)),
        scratch_shapes=[pltpu.VMEM((1, LANES), F32)],
        compiler_params=_cparams(("arbitrary",)),
        name="merge_route",
    )(y_sb, y_ret, proj, proj, x2, wsb, wret, wout, g1, b1, wr_hi, wr_lo, br)


PAD_CHUNKS = tuple(1 << b for b in reversed(range(SUBLANES.bit_length() - 1,
                                                  EXPERT_BLOCK.bit_length() - 1)))


ISSUE_UNROLL = 4


def _dispatch_kernel(padstart_ref, padlen_ref, slot_ref, x_ref, xs_hbm,
                     zero_ref, sem, zsem, *, tm):
    def pad_copies(act):
        def zero_copy(off, size):
            return pltpu.make_async_copy(zero_ref.at[pl.ds(0, size)],
                                         xs_hbm.at[pl.ds(off, size)], zsem)

        def per_expert(e, c):
            n = padlen_ref[e]
            off = padstart_ref[e]
            for r in range(SUBLANES - 1):
                @pl.when(r < (n & (SUBLANES - 1)))
                def _():
                    act(zero_copy(off + r, 1))
            off = pl.multiple_of(off + (n & (SUBLANES - 1)), SUBLANES)
            for size in PAD_CHUNKS:
                @pl.when((n & size) != 0)
                def _():
                    act(zero_copy(off, size))
                off = pl.multiple_of(off + (n & size), SUBLANES)
            return c

        lax.fori_loop(0, N_EXPERTS, per_expert, 0)

        def tail_chunk(j, c):
            off = pl.multiple_of(padstart_ref[N_EXPERTS] + j * PAD_CHUNKS[0], PAD_CHUNKS[0])
            act(zero_copy(off, PAD_CHUNKS[0]))
            return c

        lax.fori_loop(0, padlen_ref[N_EXPERTS] // PAD_CHUNKS[0], tail_chunk, 0)

    @pl.when(pl.program_id(0) == 0)
    def _():
        zero_ref[...] = jnp.zeros_like(zero_ref)
        pad_copies(lambda cp: cp.start())
        pad_copies(lambda cp: cp.wait())

    def issue(t, c):
        for k in range(TOP_K):
            pltpu.make_async_copy(x_ref.at[pl.ds(t, 1)],
                                  xs_hbm.at[pl.ds(slot_ref[k, t], 1)], sem).start()
        return c

    lax.fori_loop(0, tm, issue, 0, unroll=ISSUE_UNROLL)
    for _ in range(TOP_K):
        pltpu.make_async_copy(x_ref, xs_hbm.at[pl.ds(0, tm)], sem).wait()


def _dispatch(pad_start, pad_len, slots, x1, n_rows, tm):
    n, d = x1.shape
    grid_spec = pltpu.PrefetchScalarGridSpec(
        num_scalar_prefetch=2,
        grid=(n // tm,),
        in_specs=[pl.BlockSpec((TOP_K, tm), lambda i, *_: (0, i), memory_space=pltpu.SMEM),
                  pl.BlockSpec((tm, d), lambda i, *_: (i, 0))],
        out_specs=pl.BlockSpec(memory_space=pl.ANY),
        scratch_shapes=[pltpu.VMEM((PAD_CHUNKS[0], d), x1.dtype),
                        pltpu.SemaphoreType.DMA, pltpu.SemaphoreType.DMA])
    return pl.pallas_call(
        functools.partial(_dispatch_kernel, tm=tm),
        out_shape=jax.ShapeDtypeStruct((n_rows, d), x1.dtype),
        grid_spec=grid_spec,
        compiler_params=_cparams(("arbitrary",)),
        name="dispatch",
    )(pad_start, pad_len, slots, x1)


def _expert_kernel(be_ref, nb_ref, xs_ref, wgu_ref, bgu_ref, wd_ref, bd_ref, o_ref):
    del be_ref
    i = pl.program_id(0)

    @pl.when(i < nb_ref[0])
    def _():
        hgu = jnp.dot(xs_ref[...].astype(BF16), wgu_ref[...],
                      preferred_element_type=F32) + bgu_ref[...]
        d_ff = hgu.shape[1] // 2
        gate = jnp.minimum(hgu[:, :d_ff], SWIGLU_LIMIT)
        up = jnp.clip(hgu[:, d_ff:], -SWIGLU_LIMIT, SWIGLU_LIMIT)
        glu = gate * jax.nn.sigmoid(gate * SWIGLU_ALPHA)
        act = ((up + 1.0) * glu).astype(BF16)
        o_ref[...] = jnp.dot(act, wd_ref[...], preferred_element_type=F32) + bd_ref[...]

    @pl.when(i >= nb_ref[0])
    def _():
        o_ref[...] = jnp.zeros_like(o_ref)


def _experts(block_expert, n_used, xs, wgu, bgu, wd, bd):
    n_rows, w = xs.shape
    d = wd.shape[2]
    n_blocks = n_rows // EXPERT_BLOCK
    grid_spec = pltpu.PrefetchScalarGridSpec(
        num_scalar_prefetch=2,
        grid=(n_blocks,),
        in_specs=[pl.BlockSpec((EXPERT_BLOCK, w), lambda i, be, nb: (jnp.minimum(i, nb[0] - 1), 0)),
                  pl.BlockSpec((None,) + wgu.shape[1:], lambda i, be, nb: (be[i], 0, 0)),
                  pl.BlockSpec((None,) + bgu.shape[1:], lambda i, be, nb: (be[i], 0, 0)),
                  pl.BlockSpec((None,) + wd.shape[1:], lambda i, be, nb: (be[i], 0, 0)),
                  pl.BlockSpec((None,) + bd.shape[1:], lambda i, be, nb: (be[i], 0, 0))],
        out_specs=pl.BlockSpec((EXPERT_BLOCK, d), lambda i, be, nb: (i, 0)))
    return pl.pallas_call(
        _expert_kernel,
        out_shape=jax.ShapeDtypeStruct((n_rows, d), F32),
        grid_spec=grid_spec,
        compiler_params=_cparams(("arbitrary",)),
        name="experts",
    )(block_expert, n_used, xs, wgu, bgu, wd, bd)


def _combine_kernel(slot_ref, slot_next_ref,
                    x1_ref, topw_ref, g2_ref, b2_ref, rows_hbm,
                    o_ref, buf_ref, sem, *, tc):
    i = pl.program_id(0)
    slot = i % 2

    def issue_tile(s_ref, sl):
        def issue(t, c):
            for k in range(TOP_K):
                pltpu.make_async_copy(rows_hbm.at[pl.ds(s_ref[k, t], 1)],
                                      buf_ref.at[sl, k, pl.ds(t, 1)], sem.at[sl]).start()
            return c

        lax.fori_loop(0, tc, issue, 0, unroll=ISSUE_UNROLL)

    @pl.when(i == 0)
    def _():
        issue_tile(slot_ref, 0)

    @pl.when(i + 1 < pl.num_programs(0))
    def _():
        issue_tile(slot_next_ref, 1 - slot)

    for k in range(TOP_K):
        pltpu.make_async_copy(rows_hbm.at[pl.ds(0, tc)], buf_ref.at[slot, k], sem.at[slot]).wait()

    topw = topw_ref[...]
    y = topw[:, 0:1] * buf_ref[slot, 0]
    for k in range(1, TOP_K):
        y = y + topw[:, k:k + 1] * buf_ref[slot, k]
    o_ref[...] = _layer_norm(DEEPNORM_ALPHA * x1_ref[...] + y, g2_ref[...], b2_ref[...])


def _combine(slots, x1, topw, g2, b2, rows, tc):
    n, d = x1.shape
    nt = n // tc
    return pl.pallas_call(
        functools.partial(_combine_kernel, tc=tc),
        out_shape=jax.ShapeDtypeStruct((n, d), F32),
        grid=(nt,),
        in_specs=[pl.BlockSpec((TOP_K, tc), lambda i: (0, i), memory_space=pltpu.SMEM),
                  pl.BlockSpec((TOP_K, tc), lambda i: (0, jnp.minimum(i + 1, nt - 1)),
                               memory_space=pltpu.SMEM),
                  pl.BlockSpec((tc, d), lambda i: (i, 0)),
                  pl.BlockSpec((tc, LANES), lambda i: (i, 0)),
                  pl.BlockSpec((1, d), lambda i: (0, 0)),
                  pl.BlockSpec((1, d), lambda i: (0, 0)),
                  pl.BlockSpec(memory_space=pl.ANY)],
        out_specs=pl.BlockSpec((tc, d), lambda i: (i, 0)),
        scratch_shapes=[pltpu.VMEM((2, TOP_K, tc, d), F32), pltpu.SemaphoreType.DMA((2,))],
        compiler_params=_cparams(("arbitrary",)),
        name="combine",
    )(slots, slots, x1, topw, g2, b2, rows)


def _tile(n, pref):
    t = min(n, pref)
    assert n % t == 0, (n, t)
    return t


def kernel(x, w_in, w_branch_sb, w_branch_ret, w_out, ln1_g, ln1_b, w_router, b_router,
           w_gate_up, b_gate_up, w_down, b_down, ln2_g, ln2_b):
    b, s, d = x.shape
    n = b * s
    sb_w = SB_HEADS * SB_HEAD_DIM
    qk_w = RET_HEADS * RET_QK_DIM
    v_w = RET_HEADS * RET_V_DIM
    sizes = (sb_w, sb_w, sb_w, qk_w, qk_w, v_w, v_w, d, d)
    offs = [int(o) for o in np.cumsum((0,) + sizes)]
    assert w_in.shape[0] == DEPTH and w_in.shape[2] == offs[-1]
    assert d == sb_w and d % (2 * LANES) == 0

    half = RET_QK_DIM // 2
    inv_freq = ROPE_BASE ** (-jnp.arange(half, dtype=F32) / half)
    ang = jnp.arange(s, dtype=F32)[:, None] * inv_freq[None, :]
    cos, sin = jnp.cos(ang), jnp.sin(ang)

    n_assign = n * TOP_K
    n_rows = -(-n_assign // EXPERT_BLOCK) * EXPERT_BLOCK + N_EXPERTS * EXPERT_BLOCK
    n_blocks = n_rows // EXPERT_BLOCK

    col_scale = jnp.ones((1, offs[-1]), F32).at[:, :sb_w].set(SB_HEAD_DIM ** -0.5 * LOG2_E)

    xcur = x.reshape(n, d)
    for l in range(DEPTH):
        proj = _in_proj(xcur, w_in[l].astype(BF16), col_scale, _tile(n, 1024), _tile(offs[-1], 1024))
        proj3 = proj.reshape(b, s, offs[-1])
        y_sb = _sb_attention(proj3, _tile(s, 128), SB_HEADS)
        y_ret = _retention(proj3, cos, sin, _tile(s, 512), offs[3], offs[4], offs[5], offs[6])

        wr = jnp.zeros((d, LANES), F32).at[:, :N_EXPERTS].set(w_router[l])
        br = jnp.zeros((1, LANES), F32).at[0, :N_EXPERTS].set(b_router[l])
        wr_hi = wr.astype(BF16)
        wr_lo = (wr - wr_hi.astype(F32)).astype(BF16)
        x1, route, topw, cnt = _merge_route(
            y_sb.reshape(n, sb_w), y_ret.reshape(n, v_w), proj, xcur,
            w_branch_sb[l].astype(BF16), w_branch_ret[l].astype(BF16), w_out[l].astype(BF16),
            ln1_g[l].reshape(1, d), ln1_b[l].reshape(1, d), wr_hi, wr_lo, br,
            _tile(n, 512), offs[7], offs[8])

        counts = cnt[0, :N_EXPERTS]
        padded = (counts + EXPERT_BLOCK - 1) // EXPERT_BLOCK * EXPERT_BLOCK
        padded_ends = jnp.cumsum(padded)
        starts = (padded_ends - padded).astype(jnp.int32)
        n_used = (padded_ends[-1:] // EXPERT_BLOCK).astype(jnp.int32)
        block_rows = jnp.arange(n_blocks, dtype=jnp.int32) * EXPERT_BLOCK
        block_expert = jnp.minimum(
            jnp.sum((padded_ends[None, :] <= block_rows[:, None]).astype(jnp.int32), axis=1),
            N_EXPERTS - 1).astype(jnp.int32)
        pad_start = jnp.concatenate([starts + counts, padded_ends[-1:]]).astype(jnp.int32)
        pad_len = jnp.concatenate([padded - counts, n_rows - padded_ends[-1:]]).astype(jnp.int32)

        slots = starts[route[:TOP_K]] + route[TOP_K:]

        xs = _dispatch(pad_start, pad_len, slots, x1, n_rows, _tile(n, 512))
        rows = _experts(block_expert, n_used, xs,
                        w_gate_up[l].astype(BF16), b_gate_up[l][:, None, :],
                        w_down[l].astype(BF16), b_down[l][:, None, :])
        xcur = _combine(slots, x1, topw, ln2_g[l].reshape(1, d),
                        ln2_b[l].reshape(1, d), rows, _tile(n, 256))
    return xcur.reshape(b, s, d)
```

```python
import functools

import numpy as np
import jax
import jax.numpy as jnp
from jax import lax
from jax.experimental import pallas as pl
from jax.experimental.pallas import tpu as pltpu

F32 = jnp.float32
BF16 = jnp.bfloat16

SB_HEADS = 16
SB_HEAD_DIM = 64
RET_HEADS = 4
RET_QK_DIM = 256
RET_V_DIM = 512
RET_CHUNK = 64
ROPE_BASE = 10000.0
N_EXPERTS = 32
TOP_K = 4
EXPERT_BLOCK = 512
SWIGLU_LIMIT = 7.0
SWIGLU_ALPHA = 1.702
LN_EPS = 1e-5
DEPTH = 1
DEEPNORM_ALPHA = (2.0 * DEPTH) ** 0.25

LANES = 128
SUBLANES = 8
VMEM_LIMIT = 56 * 1024 * 1024

EXP2_UNDERFLOW_BITS = 151.0
LOG2_E = 1.4426950408889634


def _cparams(sem):
    return pltpu.CompilerParams(dimension_semantics=sem, vmem_limit_bytes=VMEM_LIMIT)


def _proj_kernel(x_ref, w_ref, cs_ref, o_ref, xb_ref):
    @pl.when(pl.program_id(1) == 0)
    def _():
        xb_ref[...] = x_ref[...].astype(BF16)

    acc = jnp.dot(xb_ref[...], w_ref[...], preferred_element_type=F32)
    o_ref[...] = (acc * cs_ref[...]).astype(o_ref.dtype)


def _in_proj(x2, w_bf, col_scale, tm, tn):
    n, d = x2.shape
    width = w_bf.shape[1]
    return pl.pallas_call(
        _proj_kernel,
        out_shape=jax.ShapeDtypeStruct((n, width), BF16),
        grid=(n // tm, width // tn),
        in_specs=[pl.BlockSpec((tm, d), lambda i, j: (i, 0)),
                  pl.BlockSpec((d, tn), lambda i, j: (0, j)),
                  pl.BlockSpec((1, tn), lambda i, j: (0, j))],
        out_specs=pl.BlockSpec((tm, tn), lambda i, j: (i, j)),
        scratch_shapes=[pltpu.VMEM((tm, d), BF16)],
        compiler_params=_cparams(("parallel", "arbitrary")),
        name="in_proj",
    )(x2, w_bf, col_scale)


def _sb_kernel(q_ref, k_ref, v_ref, o_ref, acc_ref, carry_ref, *, tq, n_heads):
    qi = pl.program_id(2)
    n_groups = n_heads // 2
    lane = lax.broadcasted_iota(jnp.int32, (tq, LANES), 1)
    low = lane < SB_HEAD_DIM
    q = q_ref[...]
    zero = jnp.zeros((tq, LANES), BF16)
    q_pairs = []
    for g in range(n_groups):
        qg = q[:, g * LANES:(g + 1) * LANES]
        q_pairs.append(jnp.concatenate([jnp.where(low, qg, zero), jnp.where(low, zero, qg)], axis=0))
    row = lax.broadcasted_iota(jnp.int32, (tq, tq), 0)
    col = lax.broadcasted_iota(jnp.int32, (tq, tq), 1)
    suffix = (row >= col).astype(BF16)
    rows_all = lax.broadcasted_iota(jnp.int32, (n_heads * tq, tq), 0)
    cols_all = lax.broadcasted_iota(jnp.int32, (n_heads * tq, tq), 1)
    strict = cols_all < (rows_all & (tq - 1))

    acc_ref[...] = jnp.zeros_like(acc_ref)
    carry_ref[...] = jnp.zeros_like(carry_ref)

    def block(kb, diag):
        start = pl.multiple_of(kb * tq, tq)
        z = jnp.concatenate(
            [lax.dot_general(q_pairs[g], k_ref[pl.ds(start, tq), pl.ds(g * LANES, LANES)],
                             (((1,), (1,)), ((), ())), preferred_element_type=F32)
             for g in range(n_groups)], axis=0)
        sp = jnp.maximum(z, 0.0) + jnp.log2(1.0 + jnp.exp2(-jnp.abs(z)))
        if diag:
            sp = jnp.where(strict, sp, 0.0)
        cum = jnp.dot(sp.astype(BF16), suffix, preferred_element_type=F32)
        carry = carry_ref[...]
        a = jnp.exp2(z - cum - carry)
        if diag:
            a = jnp.where(strict, a, 0.0)
        a = a.astype(BF16)
        pv = jnp.concatenate(
            [jnp.dot(a[g * 2 * tq:(g + 1) * 2 * tq], v_ref[pl.ds(start, tq), pl.ds(g * LANES, LANES)],
                     preferred_element_type=F32)
             for g in range(n_groups)], axis=0)
        acc_ref[...] += pv
        carry_ref[...] = carry + cum[:, 0:1]

    def alive():
        return jnp.min(carry_ref[...]) <= EXP2_UNDERFLOW_BITS

    block(qi, True)

    def cond(c):
        kb, live = c
        return jnp.logical_and(kb >= 0, live)

    def body(c):
        kb, _ = c
        block(kb, False)
        return kb - 1, alive()

    lax.while_loop(cond, body, (qi - 1, alive()))
    for g in range(n_groups):
        o_ref[:, g * LANES:(g + 1) * LANES] = jnp.where(
            low, acc_ref[pl.ds(2 * g * tq, tq), :], acc_ref[pl.ds((2 * g + 1) * tq, tq), :]
        ).astype(o_ref.dtype)


def _sb_attention(proj3, tq, n_heads):
    b, s, _ = proj3.shape
    width = n_heads * SB_HEAD_DIM
    groups = SB_HEADS // n_heads
    resident = pl.Buffered(1)
    return pl.pallas_call(
        functools.partial(_sb_kernel, tq=tq, n_heads=n_heads),
        out_shape=jax.ShapeDtypeStruct((b, s, SB_HEADS * SB_HEAD_DIM), BF16),
        grid=(b, groups, s // tq),
        in_specs=[pl.BlockSpec((None, tq, width), lambda bi, g, i: (bi, i, g)),
                  pl.BlockSpec((None, s, width), lambda bi, g, i: (bi, 0, groups + g),
                               pipeline_mode=resident),
                  pl.BlockSpec((None, s, width), lambda bi, g, i: (bi, 0, 2 * groups + g),
                               pipeline_mode=resident)],
        out_specs=pl.BlockSpec((None, tq, width), lambda bi, g, i: (bi, i, g)),
        scratch_shapes=[pltpu.VMEM((n_heads * tq, LANES), F32), pltpu.VMEM((n_heads * tq, 1), F32)],
        compiler_params=_cparams(("parallel", "parallel", "arbitrary")),
        name="sb_attn",
    )(proj3, proj3, proj3)


def _ret_kernel(lg_ref, q_ref, k_ref, v_ref, g_ref, cos_ref, sin_ref, o_ref,
                state_ref, mask_ref, qdec_ref, kdec_ref, *, tb):
    h = pl.program_id(1)
    si = pl.program_id(2)
    lg = lg_ref[h]
    half = RET_QK_DIM // 2

    @pl.when(si == 0)
    def _():
        state_ref[...] = jnp.zeros_like(state_ref)
        i = lax.broadcasted_iota(jnp.int32, (tb, tb), 0)
        j = lax.broadcasted_iota(jnp.int32, (tb, tb), 1)
        dist = jnp.abs(i - j).astype(F32)
        visible = (j // RET_CHUNK) <= (i // RET_CHUNK)
        mask_ref[...] = jnp.where(visible, jnp.exp(lg * dist), 0.0)
        r = lax.broadcasted_iota(jnp.int32, (tb, 1), 0).astype(F32)
        qdec_ref[...] = jnp.exp(lg * (r + 1.0))
        kdec_ref[...] = jnp.exp(lg * (tb - 1.0 - r))

    cos = cos_ref[...]
    sin = sin_ref[...]

    def rot(t):
        t1 = t[:, :half]
        t2 = t[:, half:]
        return jnp.concatenate([t1 * cos - t2 * sin, t1 * sin + t2 * cos], axis=-1)

    qr = rot(q_ref[...].astype(F32)) * (RET_QK_DIM ** -0.5)
    kr = rot(k_ref[...].astype(F32))
    v = v_ref[...]
    scores = lax.dot_general(qr.astype(BF16), kr.astype(BF16), (((1,), (1,)), ((), ())),
                             preferred_element_type=F32) * mask_ref[...]
    state = state_ref[...]
    y = (jnp.dot(scores.astype(BF16), v, preferred_element_type=F32)
         + jnp.dot((qr * qdec_ref[...]).astype(BF16), state.astype(BF16),
                   preferred_element_type=F32))
    kd = (kr * kdec_ref[...]).astype(BF16)
    state_ref[...] = state * jnp.exp(lg * tb) + lax.dot_general(
        kd, v, (((0,), (0,)), ((), ())), preferred_element_type=F32)

    mu = jnp.mean(y, axis=-1, keepdims=True)
    yc = y - mu
    var = jnp.mean(yc * yc, axis=-1, keepdims=True)
    yn = yc * lax.rsqrt(var + LN_EPS)
    g = g_ref[...].astype(F32)
    o_ref[...] = (g * jax.nn.sigmoid(g) * yn).astype(o_ref.dtype)


def _retention(proj3, cos, sin, tb, off_q, off_k, off_v, off_g):
    b, s, _ = proj3.shape
    log_gamma = jnp.asarray(np.log(1.0 - 2.0 ** (-5.0 - np.arange(RET_HEADS))), F32)
    qb, kb = off_q // RET_QK_DIM, off_k // RET_QK_DIM
    vb, gb = off_v // RET_V_DIM, off_g // RET_V_DIM
    half = RET_QK_DIM // 2
    grid_spec = pltpu.PrefetchScalarGridSpec(
        num_scalar_prefetch=1,
        grid=(b, RET_HEADS, s // tb),
        in_specs=[pl.BlockSpec((None, tb, RET_QK_DIM), lambda bi, h, i, lg: (bi, i, qb + h)),
                  pl.BlockSpec((None, tb, RET_QK_DIM), lambda bi, h, i, lg: (bi, i, kb + h)),
                  pl.BlockSpec((None, tb, RET_V_DIM), lambda bi, h, i, lg: (bi, i, vb + h)),
                  pl.BlockSpec((None, tb, RET_V_DIM), lambda bi, h, i, lg: (bi, i, gb + h)),
                  pl.BlockSpec((tb, half), lambda bi, h, i, lg: (i, 0)),
                  pl.BlockSpec((tb, half), lambda bi, h, i, lg: (i, 0))],
        out_specs=pl.BlockSpec((None, tb, RET_V_DIM), lambda bi, h, i, lg: (bi, i, h)),
        scratch_shapes=[pltpu.VMEM((RET_QK_DIM, RET_V_DIM), F32),
                        pltpu.VMEM((tb, tb), F32),
                        pltpu.VMEM((tb, 1), F32),
                        pltpu.VMEM((tb, 1), F32)])
    return pl.pallas_call(
        functools.partial(_ret_kernel, tb=tb),
        out_shape=jax.ShapeDtypeStruct((b, s, RET_HEADS * RET_V_DIM), BF16),
        grid_spec=grid_spec,
        compiler_params=_cparams(("parallel", "parallel", "arbitrary")),
        name="retention",
    )(log_gamma, proj3, proj3, proj3, proj3, cos, sin)


def _layer_norm(h, g, b):
    mu = jnp.mean(h, axis=-1, keepdims=True)
    hc = h - mu
    var = jnp.mean(hc * hc, axis=-1, keepdims=True)
    return hc * lax.rsqrt(var + LN_EPS) * g + b


def _merge_kernel(ysb_ref, yret_ref, gsb_ref, gret_ref, x_ref, wsb_ref, wret_ref, wout_ref,
                  g1_ref, b1_ref, wrh_ref, wrl_ref, br_ref,
                  x1_ref, route_ref, topw_ref, cnt_ref, run_ref, *, tm):
    i = pl.program_id(0)

    @pl.when(i == 0)
    def _():
        run_ref[...] = jnp.zeros_like(run_ref)

    bsb = jnp.dot(ysb_ref[...], wsb_ref[...], preferred_element_type=F32)
    bret = jnp.dot(yret_ref[...], wret_ref[...], preferred_element_type=F32)
    merged = (jax.nn.sigmoid(gsb_ref[...].astype(F32)) * bsb
              + jax.nn.sigmoid(gret_ref[...].astype(F32)) * bret)
    mix = jnp.dot(merged.astype(BF16), wout_ref[...], preferred_element_type=F32)
    x1 = _layer_norm(DEEPNORM_ALPHA * x_ref[...] + mix, g1_ref[...], b1_ref[...])
    x1_ref[...] = x1

    x_hi = x1.astype(BF16)
    x_lo = (x1 - x_hi.astype(F32)).astype(BF16)
    logits = (jnp.dot(x_hi, wrh_ref[...], preferred_element_type=F32)
              + jnp.dot(x_hi, wrl_ref[...], preferred_element_type=F32)
              + jnp.dot(x_lo, wrh_ref[...], preferred_element_type=F32)) + br_ref[...]
    lane = lax.broadcasted_iota(jnp.int32, (tm, LANES), 1)
    lane_f = lane.astype(F32)
    neg = jnp.asarray(-jnp.inf, F32)
    work = jnp.where(lane < N_EXPERTS, logits, neg)
    vals, idxs = [], []
    sel = jnp.zeros((tm, LANES), F32)
    for _ in range(TOP_K):
        m = jnp.max(work, axis=-1, keepdims=True)
        idx = jnp.min(jnp.where(work == m, lane_f, float(LANES)), axis=-1, keepdims=True)
        hit = lane_f == idx
        sel = jnp.where(hit, 1.0, sel)
        work = jnp.where(hit, neg, work)
        vals.append(m)
        idxs.append(idx)
    exps = [jnp.exp(v - vals[0]) for v in vals]
    denom = exps[0] + exps[1] + exps[2] + exps[3]
    weights = [e / denom for e in exps]

    r = lax.broadcasted_iota(jnp.int32, (tm, tm), 0)
    c = lax.broadcasted_iota(jnp.int32, (tm, tm), 1)
    before = (c < r).astype(BF16)
    rank_mat = jnp.dot(before, sel.astype(BF16), preferred_element_type=F32) + run_ref[...]
    route = jnp.zeros((tm, LANES), F32)
    topw = jnp.zeros((tm, LANES), F32)
    for k in range(TOP_K):
        rank_k = jnp.sum(jnp.where(lane_f == idxs[k], rank_mat, 0.0), axis=-1, keepdims=True)
        route = jnp.where(lane == k, idxs[k], route)
        route = jnp.where(lane == TOP_K + k, rank_k, route)
        topw = jnp.where(lane == k, weights[k], topw)
    route_ref[...] = route.T[:2 * TOP_K, :].astype(jnp.int32)
    topw_ref[...] = topw
    run = run_ref[...] + jnp.sum(sel, axis=0, keepdims=True)
    run_ref[...] = run
    cnt_ref[...] = jnp.broadcast_to(run, cnt_ref.shape).astype(jnp.int32)


def _merge_route(y_sb, y_ret, proj, x2, wsb, wret, wout, g1, b1, wr_hi, wr_lo, br, tm,
                 off_gsb, off_gret):
    n, d = x2.shape
    const = lambda i: (0, 0)
    row = lambda i: (i, 0)
    return pl.pallas_call(
        functools.partial(_merge_kernel, tm=tm),
        out_shape=(jax.ShapeDtypeStruct((n, d), F32),
                   jax.ShapeDtypeStruct((2 * TOP_K, n), jnp.int32),
                   jax.ShapeDtypeStruct((n, LANES), F32),
                   jax.ShapeDtypeStruct((8, LANES), jnp.int32)),
        grid=(n // tm,),
        in_specs=[pl.BlockSpec((tm, y_sb.shape[1]), row),
                  pl.BlockSpec((tm, y_ret.shape[1]), row),
                  pl.BlockSpec((tm, d), lambda i: (i, off_gsb // d)),
                  pl.BlockSpec((tm, d), lambda i: (i, off_gret // d)),
                  pl.BlockSpec((tm, d), row),
                  pl.BlockSpec(wsb.shape, const),
                  pl.BlockSpec(wret.shape, const),
                  pl.BlockSpec(wout.shape, const),
                  pl.BlockSpec((1, d), const),
                  pl.BlockSpec((1, d), const),
                  pl.BlockSpec(wr_hi.shape, const),
                  pl.BlockSpec(wr_lo.shape, const),
                  pl.BlockSpec((1, LANES), const)],
        out_specs=(pl.BlockSpec((tm, d), row),
                   pl.BlockSpec((2 * TOP_K, tm), lambda i: (0, i)),
                   pl.BlockSpec((tm, LANES), row),
                   pl.BlockSpec((8, LANES), const)),
        scratch_shapes=[pltpu.VMEM((1, LANES), F32)],
        compiler_params=_cparams(("arbitrary",)),
        name="merge_route",
    )(y_sb, y_ret, proj, proj, x2, wsb, wret, wout, g1, b1, wr_hi, wr_lo, br)


PAD_CHUNKS = tuple(1 << b for b in reversed(range(SUBLANES.bit_length() - 1,
                                                  EXPERT_BLOCK.bit_length() - 1)))


ISSUE_UNROLL = 4


def _dispatch_kernel(padstart_ref, padlen_ref, slot_ref, x_ref, xs_hbm,
                     zero_ref, sem, zsem, *, tm):
    def pad_copies(act):
        def zero_copy(off, size):
            return pltpu.make_async_copy(zero_ref.at[pl.ds(0, size)],
                                         xs_hbm.at[pl.ds(off, size)], zsem)

        def per_expert(e, c):
            n = padlen_ref[e]
            off = padstart_ref[e]
            for r in range(SUBLANES - 1):
                @pl.when(r < (n & (SUBLANES - 1)))
                def _():
                    act(zero_copy(off + r, 1))
            off = pl.multiple_of(off + (n & (SUBLANES - 1)), SUBLANES)
            for size in PAD_CHUNKS:
                @pl.when((n & size) != 0)
                def _():
                    act(zero_copy(off, size))
                off = pl.multiple_of(off + (n & size), SUBLANES)
            return c

        lax.fori_loop(0, N_EXPERTS, per_expert, 0)

        def tail_chunk(j, c):
            off = pl.multiple_of(padstart_ref[N_EXPERTS] + j * PAD_CHUNKS[0], PAD_CHUNKS[0])
            act(zero_copy(off, PAD_CHUNKS[0]))
            return c

        lax.fori_loop(0, padlen_ref[N_EXPERTS] // PAD_CHUNKS[0], tail_chunk, 0)

    @pl.when(pl.program_id(0) == 0)
    def _():
        zero_ref[...] = jnp.zeros_like(zero_ref)
        pad_copies(lambda cp: cp.start())
        pad_copies(lambda cp: cp.wait())

    def issue(t, c):
        for k in range(TOP_K):
            pltpu.make_async_copy(x_ref.at[pl.ds(t, 1)],
                                  xs_hbm.at[pl.ds(slot_ref[k, t], 1)], sem).start()
        return c

    lax.fori_loop(0, tm, issue, 0, unroll=ISSUE_UNROLL)
    for _ in range(TOP_K):
        pltpu.make_async_copy(x_ref, xs_hbm.at[pl.ds(0, tm)], sem).wait()


def _dispatch(pad_start, pad_len, slots, x1, n_rows, tm):
    n, d = x1.shape
    grid_spec = pltpu.PrefetchScalarGridSpec(
        num_scalar_prefetch=2,
        grid=(n // tm,),
        in_specs=[pl.BlockSpec((TOP_K, tm), lambda i, *_: (0, i), memory_space=pltpu.SMEM),
                  pl.BlockSpec((tm, d), lambda i, *_: (i, 0))],
        out_specs=pl.BlockSpec(memory_space=pl.ANY),
        scratch_shapes=[pltpu.VMEM((PAD_CHUNKS[0], d), x1.dtype),
                        pltpu.SemaphoreType.DMA, pltpu.SemaphoreType.DMA])
    return pl.pallas_call(
        functools.partial(_dispatch_kernel, tm=tm),
        out_shape=jax.ShapeDtypeStruct((n_rows, d), x1.dtype),
        grid_spec=grid_spec,
        compiler_params=_cparams(("arbitrary",)),
        name="dispatch",
    )(pad_start, pad_len, slots, x1)


def _expert_kernel(be_ref, nb_ref, xs_ref, wgu_ref, bgu_ref, wd_ref, bd_ref, o_ref):
    del be_ref
    i = pl.program_id(0)

    @pl.when(i < nb_ref[0])
    def _():
        hgu = jnp.dot(xs_ref[...].astype(BF16), wgu_ref[...],
                      preferred_element_type=F32) + bgu_ref[...]
        d_ff = hgu.shape[1] // 2
        gate = jnp.minimum(hgu[:, :d_ff], SWIGLU_LIMIT)
        up = jnp.clip(hgu[:, d_ff:], -SWIGLU_LIMIT, SWIGLU_LIMIT)
        glu = gate * jax.nn.sigmoid(gate * SWIGLU_ALPHA)
        act = ((up + 1.0) * glu).astype(BF16)
        o_ref[...] = jnp.dot(act, wd_ref[...], preferred_element_type=F32) + bd_ref[...]

    @pl.when(i >= nb_ref[0])
    def _():
        o_ref[...] = jnp.zeros_like(o_ref)


def _experts(block_expert, n_used, xs, wgu, bgu, wd, bd):
    n_rows, w = xs.shape
    d = wd.shape[2]
    n_blocks = n_rows // EXPERT_BLOCK
    grid_spec = pltpu.PrefetchScalarGridSpec(
        num_scalar_prefetch=2,
        grid=(n_blocks,),
        in_specs=[pl.BlockSpec((EXPERT_BLOCK, w), lambda i, be, nb: (jnp.minimum(i, nb[0] - 1), 0)),
                  pl.BlockSpec((None,) + wgu.shape[1:], lambda i, be, nb: (be[i], 0, 0)),
                  pl.BlockSpec((None,) + bgu.shape[1:], lambda i, be, nb: (be[i], 0, 0)),
                  pl.BlockSpec((None,) + wd.shape[1:], lambda i, be, nb: (be[i], 0, 0)),
                  pl.BlockSpec((None,) + bd.shape[1:], lambda i, be, nb: (be[i], 0, 0))],
        out_specs=pl.BlockSpec((EXPERT_BLOCK, d), lambda i, be, nb: (i, 0)))
    return pl.pallas_call(
        _expert_kernel,
        out_shape=jax.ShapeDtypeStruct((n_rows, d), F32),
        grid_spec=grid_spec,
        compiler_params=_cparams(("arbitrary",)),
        name="experts",
    )(block_expert, n_used, xs, wgu, bgu, wd, bd)


def _combine_kernel(slot_ref, slot_next_ref,
                    x1_ref, topw_ref, g2_ref, b2_ref, rows_hbm,
                    o_ref, buf_ref, sem, *, tc):
    i = pl.program_id(0)
    slot = i % 2

    def issue_tile(s_ref, sl):
        def issue(t, c):
            for k in range(TOP_K):
                pltpu.make_async_copy(rows_hbm.at[pl.ds(s_ref[k, t], 1)],
                                      buf_ref.at[sl, k, pl.ds(t, 1)], sem.at[sl]).start()
            return c

        lax.fori_loop(0, tc, issue, 0, unroll=ISSUE_UNROLL)

    @pl.when(i == 0)
    def _():
        issue_tile(slot_ref, 0)

    @pl.when(i + 1 < pl.num_programs(0))
    def _():
        issue_tile(slot_next_ref, 1 - slot)

    for k in range(TOP_K):
        pltpu.make_async_copy(rows_hbm.at[pl.ds(0, tc)], buf_ref.at[slot, k], sem.at[slot]).wait()

    topw = topw_ref[...]
    y = topw[:, 0:1] * buf_ref[slot, 0]
    for k in range(1, TOP_K):
        y = y + topw[:, k:k + 1] * buf_ref[slot, k]
    o_ref[...] = _layer_norm(DEEPNORM_ALPHA * x1_ref[...] + y, g2_ref[...], b2_ref[...])


def _combine(slots, x1, topw, g2, b2, rows, tc):
    n, d = x1.shape
    nt = n // tc
    return pl.pallas_call(
        functools.partial(_combine_kernel, tc=tc),
        out_shape=jax.ShapeDtypeStruct((n, d), F32),
        grid=(nt,),
        in_specs=[pl.BlockSpec((TOP_K, tc), lambda i: (0, i), memory_space=pltpu.SMEM),
                  pl.BlockSpec((TOP_K, tc), lambda i: (0, jnp.minimum(i + 1, nt - 1)),
                               memory_space=pltpu.SMEM),
                  pl.BlockSpec((tc, d), lambda i: (i, 0)),
                  pl.BlockSpec((tc, LANES), lambda i: (i, 0)),
                  pl.BlockSpec((1, d), lambda i: (0, 0)),
                  pl.BlockSpec((1, d), lambda i: (0, 0)),
                  pl.BlockSpec(memory_space=pl.ANY)],
        out_specs=pl.BlockSpec((tc, d), lambda i: (i, 0)),
        scratch_shapes=[pltpu.VMEM((2, TOP_K, tc, d), F32), pltpu.SemaphoreType.DMA((2,))],
        compiler_params=_cparams(("arbitrary",)),
        name="combine",
    )(slots, slots, x1, topw, g2, b2, rows)


def _tile(n, pref):
    t = min(n, pref)
    assert n % t == 0, (n, t)
    return t


def kernel(x, w_in, w_branch_sb, w_branch_ret, w_out, ln1_g, ln1_b, w_router, b_router,
           w_gate_up, b_gate_up, w_down, b_down, ln2_g, ln2_b):
    b, s, d = x.shape
    n = b * s
    sb_w = SB_HEADS * SB_HEAD_DIM
    qk_w = RET_HEADS * RET_QK_DIM
    v_w = RET_HEADS * RET_V_DIM
    sizes = (sb_w, sb_w, sb_w, qk_w, qk_w, v_w, v_w, d, d)
    offs = [int(o) for o in np.cumsum((0,) + sizes)]
    assert w_in.shape[0] == DEPTH and w_in.shape[2] == offs[-1]
    assert d == sb_w and d % (2 * LANES) == 0

    half = RET_QK_DIM // 2
    inv_freq = ROPE_BASE ** (-jnp.arange(half, dtype=F32) / half)
    ang = jnp.arange(s, dtype=F32)[:, None] * inv_freq[None, :]
    cos, sin = jnp.cos(ang), jnp.sin(ang)

    n_assign = n * TOP_K
    n_rows = -(-n_assign // EXPERT_BLOCK) * EXPERT_BLOCK + N_EXPERTS * EXPERT_BLOCK
    n_blocks = n_rows // EXPERT_BLOCK

    col_scale = jnp.ones((1, offs[-1]), F32).at[:, :sb_w].set(SB_HEAD_DIM ** -0.5 * LOG2_E)

    xcur = x.reshape(n, d)
    for l in range(DEPTH):
        proj = _in_proj(xcur, w_in[l].astype(BF16), col_scale, _tile(n, 1024), _tile(offs[-1], 1024))
        proj3 = proj.reshape(b, s, offs[-1])
        y_sb = _sb_attention(proj3, _tile(s, 128), SB_HEADS)
        y_ret = _retention(proj3, cos, sin, _tile(s, 512), offs[3], offs[4], offs[5], offs[6])

        wr = jnp.zeros((d, LANES), F32).at[:, :N_EXPERTS].set(w_router[l])
        br = jnp.zeros((1, LANES), F32).at[0, :N_EXPERTS].set(b_router[l])
        wr_hi = wr.astype(BF16)
        wr_lo = (wr - wr_hi.astype(F32)).astype(BF16)
        x1, route, topw, cnt = _merge_route(
            y_sb.reshape(n, sb_w), y_ret.reshape(n, v_w), proj, xcur,
            w_branch_sb[l].astype(BF16), w_branch_ret[l].astype(BF16), w_out[l].astype(BF16),
            ln1_g[l].reshape(1, d), ln1_b[l].reshape(1, d), wr_hi, wr_lo, br,
            _tile(n, 512), offs[7], offs[8])

        counts = cnt[0, :N_EXPERTS]
        padded = (counts + EXPERT_BLOCK - 1) // EXPERT_BLOCK * EXPERT_BLOCK
        padded_ends = jnp.cumsum(padded)
        starts = (padded_ends - padded).astype(jnp.int32)
        n_used = (padded_ends[-1:] // EXPERT_BLOCK).astype(jnp.int32)
        block_rows = jnp.arange(n_blocks, dtype=jnp.int32) * EXPERT_BLOCK
        block_expert = jnp.minimum(
            jnp.sum((padded_ends[None, :] <= block_rows[:, None]).astype(jnp.int32), axis=1),
            N_EXPERTS - 1).astype(jnp.int32)
        pad_start = jnp.concatenate([starts + counts, padded_ends[-1:]]).astype(jnp.int32)
        pad_len = jnp.concatenate([padded - counts, n_rows - padded_ends[-1:]]).astype(jnp.int32)

        expert_ids = jnp.arange(N_EXPERTS, dtype=jnp.int32)[:, None, None]
        slots = route[TOP_K:] + jnp.sum(
            jnp.where(route[None, :TOP_K] == expert_ids, starts[:, None, None], 0), axis=0)

        xs = _dispatch(pad_start, pad_len, slots, x1, n_rows, _tile(n, 512))
        rows = _experts(block_expert, n_used, xs,
                        w_gate_up[l].astype(BF16), b_gate_up[l][:, None, :],
                        w_down[l].astype(BF16), b_down[l][:, None, :])
        xcur = _combine(slots, x1, topw, ln2_g[l].reshape(1, d),
                        ln2_b[l].reshape(1, d), rows, _tile(n, 256))
    return xcur.reshape(b, s, d)
```

```python
import functools

import numpy as np
import jax
import jax.numpy as jnp
from jax import lax
from jax.experimental import pallas as pl
from jax.experimental.pallas import tpu as pltpu

F32 = jnp.float32
BF16 = jnp.bfloat16

SB_HEADS = 16
SB_HEAD_DIM = 64
RET_HEADS = 4
RET_QK_DIM = 256
RET_V_DIM = 512
RET_CHUNK = 64
ROPE_BASE = 10000.0
N_EXPERTS = 32
TOP_K = 4
EXPERT_BLOCK = 512
SWIGLU_LIMIT = 7.0
SWIGLU_ALPHA = 1.702
LN_EPS = 1e-5
DEPTH = 1
DEEPNORM_ALPHA = (2.0 * DEPTH) ** 0.25

LANES = 128
SUBLANES = 8
VMEM_LIMIT = 56 * 1024 * 1024

EXP2_UNDERFLOW_BITS = 151.0
LOG2_E = 1.4426950408889634


def _cparams(sem):
    return pltpu.CompilerParams(dimension_semantics=sem, vmem_limit_bytes=VMEM_LIMIT)


def _proj_kernel(x_ref, w_ref, cs_ref, o_ref, xb_ref):
    @pl.when(pl.program_id(1) == 0)
    def _():
        xb_ref[...] = x_ref[...].astype(BF16)

    acc = jnp.dot(xb_ref[...], w_ref[...], preferred_element_type=F32)
    o_ref[...] = (acc * cs_ref[...]).astype(o_ref.dtype)


def _in_proj(x2, w_bf, col_scale, tm, tn):
    n, d = x2.shape
    width = w_bf.shape[1]
    return pl.pallas_call(
        _proj_kernel,
        out_shape=jax.ShapeDtypeStruct((n, width), BF16),
        grid=(n // tm, width // tn),
        in_specs=[pl.BlockSpec((tm, d), lambda i, j: (i, 0)),
                  pl.BlockSpec((d, tn), lambda i, j: (0, j)),
                  pl.BlockSpec((1, tn), lambda i, j: (0, j))],
        out_specs=pl.BlockSpec((tm, tn), lambda i, j: (i, j)),
        scratch_shapes=[pltpu.VMEM((tm, d), BF16)],
        compiler_params=_cparams(("parallel", "arbitrary")),
        name="in_proj",
    )(x2, w_bf, col_scale)


def _sb_kernel(q_ref, k_ref, v_ref, o_ref, acc_ref, carry_ref, *, tq, n_heads):
    qi = pl.program_id(2)
    n_groups = n_heads // 2
    lane = lax.broadcasted_iota(jnp.int32, (tq, LANES), 1)
    low = lane < SB_HEAD_DIM
    q = q_ref[...]
    zero = jnp.zeros((tq, LANES), BF16)
    q_pairs = []
    for g in range(n_groups):
        qg = q[:, g * LANES:(g + 1) * LANES]
        q_pairs.append(jnp.concatenate([jnp.where(low, qg, zero), jnp.where(low, zero, qg)], axis=0))
    row = lax.broadcasted_iota(jnp.int32, (tq, tq), 0)
    col = lax.broadcasted_iota(jnp.int32, (tq, tq), 1)
    suffix = (row >= col).astype(BF16)
    rows_all = lax.broadcasted_iota(jnp.int32, (n_heads * tq, tq), 0)
    cols_all = lax.broadcasted_iota(jnp.int32, (n_heads * tq, tq), 1)
    strict = cols_all < (rows_all & (tq - 1))

    acc_ref[...] = jnp.zeros_like(acc_ref)
    carry_ref[...] = jnp.zeros_like(carry_ref)

    def block(kb, diag):
        start = pl.multiple_of(kb * tq, tq)
        z = jnp.concatenate(
            [lax.dot_general(q_pairs[g], k_ref[pl.ds(start, tq), pl.ds(g * LANES, LANES)],
                             (((1,), (1,)), ((), ())), preferred_element_type=F32)
             for g in range(n_groups)], axis=0)
        sp = jnp.maximum(z, 0.0) + jnp.log2(1.0 + jnp.exp2(-jnp.abs(z)))
        if diag:
            sp = jnp.where(strict, sp, 0.0)
        cum = jnp.dot(sp.astype(BF16), suffix, preferred_element_type=F32)
        carry = carry_ref[...]
        a = jnp.exp2(z - cum - carry)
        if diag:
            a = jnp.where(strict, a, 0.0)
        a = a.astype(BF16)
        pv = jnp.concatenate(
            [jnp.dot(a[g * 2 * tq:(g + 1) * 2 * tq], v_ref[pl.ds(start, tq), pl.ds(g * LANES, LANES)],
                     preferred_element_type=F32)
             for g in range(n_groups)], axis=0)
        acc_ref[...] += pv
        carry_ref[...] = carry + cum[:, 0:1]

    def alive():
        return jnp.min(carry_ref[...]) <= EXP2_UNDERFLOW_BITS

    block(qi, True)

    def cond(c):
        kb, live = c
        return jnp.logical_and(kb >= 0, live)

    def body(c):
        kb, _ = c
        block(kb, False)
        return kb - 1, alive()

    lax.while_loop(cond, body, (qi - 1, alive()))
    for g in range(n_groups):
        o_ref[:, g * LANES:(g + 1) * LANES] = jnp.where(
            low, acc_ref[pl.ds(2 * g * tq, tq), :], acc_ref[pl.ds((2 * g + 1) * tq, tq), :]
        ).astype(o_ref.dtype)


def _sb_attention(proj3, tq, n_heads):
    b, s, _ = proj3.shape
    width = n_heads * SB_HEAD_DIM
    groups = SB_HEADS // n_heads
    resident = pl.Buffered(1)
    return pl.pallas_call(
        functools.partial(_sb_kernel, tq=tq, n_heads=n_heads),
        out_shape=jax.ShapeDtypeStruct((b, s, SB_HEADS * SB_HEAD_DIM), BF16),
        grid=(b, groups, s // tq),
        in_specs=[pl.BlockSpec((None, tq, width), lambda bi, g, i: (bi, i, g)),
                  pl.BlockSpec((None, s, width), lambda bi, g, i: (bi, 0, groups + g),
                               pipeline_mode=resident),
                  pl.BlockSpec((None, s, width), lambda bi, g, i: (bi, 0, 2 * groups + g),
                               pipeline_mode=resident)],
        out_specs=pl.BlockSpec((None, tq, width), lambda bi, g, i: (bi, i, g)),
        scratch_shapes=[pltpu.VMEM((n_heads * tq, LANES), F32), pltpu.VMEM((n_heads * tq, 1), F32)],
        compiler_params=_cparams(("parallel", "parallel", "arbitrary")),
        name="sb_attn",
    )(proj3, proj3, proj3)


def _ret_kernel(lg_ref, q_ref, k_ref, v_ref, g_ref, cos_ref, sin_ref, o_ref,
                state_ref, mask_ref, qdec_ref, kdec_ref, *, tb):
    h = pl.program_id(1)
    si = pl.program_id(2)
    lg = lg_ref[h]
    half = RET_QK_DIM // 2

    @pl.when(si == 0)
    def _():
        state_ref[...] = jnp.zeros_like(state_ref)
        i = lax.broadcasted_iota(jnp.int32, (tb, tb), 0)
        j = lax.broadcasted_iota(jnp.int32, (tb, tb), 1)
        dist = jnp.abs(i - j).astype(F32)
        visible = (j // RET_CHUNK) <= (i // RET_CHUNK)
        mask_ref[...] = jnp.where(visible, jnp.exp(lg * dist), 0.0)
        r = lax.broadcasted_iota(jnp.int32, (tb, 1), 0).astype(F32)
        qdec_ref[...] = jnp.exp(lg * (r + 1.0))
        kdec_ref[...] = jnp.exp(lg * (tb - 1.0 - r))

    cos = cos_ref[...]
    sin = sin_ref[...]

    def rot(t):
        t1 = t[:, :half]
        t2 = t[:, half:]
        return jnp.concatenate([t1 * cos - t2 * sin, t1 * sin + t2 * cos], axis=-1)

    qr = rot(q_ref[...].astype(F32)) * (RET_QK_DIM ** -0.5)
    kr = rot(k_ref[...].astype(F32))
    v = v_ref[...]
    scores = lax.dot_general(qr.astype(BF16), kr.astype(BF16), (((1,), (1,)), ((), ())),
                             preferred_element_type=F32) * mask_ref[...]
    state = state_ref[...]
    y = (jnp.dot(scores.astype(BF16), v, preferred_element_type=F32)
         + jnp.dot((qr * qdec_ref[...]).astype(BF16), state.astype(BF16),
                   preferred_element_type=F32))
    kd = (kr * kdec_ref[...]).astype(BF16)
    state_ref[...] = state * jnp.exp(lg * tb) + lax.dot_general(
        kd, v, (((0,), (0,)), ((), ())), preferred_element_type=F32)

    mu = jnp.mean(y, axis=-1, keepdims=True)
    yc = y - mu
    var = jnp.mean(yc * yc, axis=-1, keepdims=True)
    yn = yc * lax.rsqrt(var + LN_EPS)
    g = g_ref[...].astype(F32)
    o_ref[...] = (g * jax.nn.sigmoid(g) * yn).astype(o_ref.dtype)


def _retention(proj3, cos, sin, tb, off_q, off_k, off_v, off_g):
    b, s, _ = proj3.shape
    log_gamma = jnp.asarray(np.log(1.0 - 2.0 ** (-5.0 - np.arange(RET_HEADS))), F32)
    qb, kb = off_q // RET_QK_DIM, off_k // RET_QK_DIM
    vb, gb = off_v // RET_V_DIM, off_g // RET_V_DIM
    half = RET_QK_DIM // 2
    grid_spec = pltpu.PrefetchScalarGridSpec(
        num_scalar_prefetch=1,
        grid=(b, RET_HEADS, s // tb),
        in_specs=[pl.BlockSpec((None, tb, RET_QK_DIM), lambda bi, h, i, lg: (bi, i, qb + h)),
                  pl.BlockSpec((None, tb, RET_QK_DIM), lambda bi, h, i, lg: (bi, i, kb + h)),
                  pl.BlockSpec((None, tb, RET_V_DIM), lambda bi, h, i, lg: (bi, i, vb + h)),
                  pl.BlockSpec((None, tb, RET_V_DIM), lambda bi, h, i, lg: (bi, i, gb + h)),
                  pl.BlockSpec((tb, half), lambda bi, h, i, lg: (i, 0)),
                  pl.BlockSpec((tb, half), lambda bi, h, i, lg: (i, 0))],
        out_specs=pl.BlockSpec((None, tb, RET_V_DIM), lambda bi, h, i, lg: (bi, i, h)),
        scratch_shapes=[pltpu.VMEM((RET_QK_DIM, RET_V_DIM), F32),
                        pltpu.VMEM((tb, tb), F32),
                        pltpu.VMEM((tb, 1), F32),
                        pltpu.VMEM((tb, 1), F32)])
    return pl.pallas_call(
        functools.partial(_ret_kernel, tb=tb),
        out_shape=jax.ShapeDtypeStruct((b, s, RET_HEADS * RET_V_DIM), BF16),
        grid_spec=grid_spec,
        compiler_params=_cparams(("parallel", "parallel", "arbitrary")),
        name="retention",
    )(log_gamma, proj3, proj3, proj3, proj3, cos, sin)


def _layer_norm(h, g, b):
    mu = jnp.mean(h, axis=-1, keepdims=True)
    hc = h - mu
    var = jnp.mean(hc * hc, axis=-1, keepdims=True)
    return hc * lax.rsqrt(var + LN_EPS) * g + b


def _merge_kernel(ysb_ref, yret_ref, gsb_ref, gret_ref, x_ref, wsb_ref, wret_ref, wout_ref,
                  g1_ref, b1_ref, wrh_ref, wrl_ref, br_ref,
                  x1_ref, route_ref, topw_ref, cnt_ref, run_ref, *, tm):
    i = pl.program_id(0)

    @pl.when(i == 0)
    def _():
        run_ref[...] = jnp.zeros_like(run_ref)

    bsb = jnp.dot(ysb_ref[...], wsb_ref[...], preferred_element_type=F32)
    bret = jnp.dot(yret_ref[...], wret_ref[...], preferred_element_type=F32)
    merged = (jax.nn.sigmoid(gsb_ref[...].astype(F32)) * bsb
              + jax.nn.sigmoid(gret_ref[...].astype(F32)) * bret)
    mix = jnp.dot(merged.astype(BF16), wout_ref[...], preferred_element_type=F32)
    x1 = _layer_norm(DEEPNORM_ALPHA * x_ref[...] + mix, g1_ref[...], b1_ref[...])
    x1_ref[...] = x1

    x_hi = x1.astype(BF16)
    x_lo = (x1 - x_hi.astype(F32)).astype(BF16)
    logits = (jnp.dot(x_hi, wrh_ref[...], preferred_element_type=F32)
              + jnp.dot(x_hi, wrl_ref[...], preferred_element_type=F32)
              + jnp.dot(x_lo, wrh_ref[...], preferred_element_type=F32)) + br_ref[...]
    lane = lax.broadcasted_iota(jnp.int32, (tm, LANES), 1)
    lane_f = lane.astype(F32)
    neg = jnp.asarray(-jnp.inf, F32)
    work = jnp.where(lane < N_EXPERTS, logits, neg)
    vals, idxs = [], []
    sel = jnp.zeros((tm, LANES), F32)
    for _ in range(TOP_K):
        m = jnp.max(work, axis=-1, keepdims=True)
        idx = jnp.min(jnp.where(work == m, lane_f, float(LANES)), axis=-1, keepdims=True)
        hit = lane_f == idx
        sel = jnp.where(hit, 1.0, sel)
        work = jnp.where(hit, neg, work)
        vals.append(m)
        idxs.append(idx)
    exps = [jnp.exp(v - vals[0]) for v in vals]
    denom = exps[0] + exps[1] + exps[2] + exps[3]
    weights = [e / denom for e in exps]

    r = lax.broadcasted_iota(jnp.int32, (tm, tm), 0)
    c = lax.broadcasted_iota(jnp.int32, (tm, tm), 1)
    before = (c < r).astype(BF16)
    rank_mat = jnp.dot(before, sel.astype(BF16), preferred_element_type=F32) + run_ref[...]
    route = jnp.zeros((tm, LANES), F32)
    topw = jnp.zeros((tm, LANES), F32)
    for k in range(TOP_K):
        rank_k = jnp.sum(jnp.where(lane_f == idxs[k], rank_mat, 0.0), axis=-1, keepdims=True)
        route = jnp.where(lane == k, idxs[k], route)
        route = jnp.where(lane == TOP_K + k, rank_k, route)
        topw = jnp.where(lane == k, weights[k], topw)
    route_ref[...] = route.T[:2 * TOP_K, :].astype(jnp.int32)
    topw_ref[...] = topw
    run = run_ref[...] + jnp.sum(sel, axis=0, keepdims=True)
    run_ref[...] = run
    cnt_ref[...] = jnp.broadcast_to(run, cnt_ref.shape).astype(jnp.int32)


def _merge_route(y_sb, y_ret, proj, x2, wsb, wret, wout, g1, b1, wr_hi, wr_lo, br, tm,
                 off_gsb, off_gret):
    n, d = x2.shape
    const = lambda i: (0, 0)
    row = lambda i: (i, 0)
    return pl.pallas_call(
        functools.partial(_merge_kernel, tm=tm),
        out_shape=(jax.ShapeDtypeStruct((n, d), F32),
                   jax.ShapeDtypeStruct((2 * TOP_K, n), jnp.int32),
                   jax.ShapeDtypeStruct((n, LANES), F32),
                   jax.ShapeDtypeStruct((8, LANES), jnp.int32)),
        grid=(n // tm,),
        in_specs=[pl.BlockSpec((tm, y_sb.shape[1]), row),
                  pl.BlockSpec((tm, y_ret.shape[1]), row),
                  pl.BlockSpec((tm, d), lambda i: (i, off_gsb // d)),
                  pl.BlockSpec((tm, d), lambda i: (i, off_gret // d)),
                  pl.BlockSpec((tm, d), row),
                  pl.BlockSpec(wsb.shape, const),
                  pl.BlockSpec(wret.shape, const),
                  pl.BlockSpec(wout.shape, const),
                  pl.BlockSpec((1, d), const),
                  pl.BlockSpec((1, d), const),
                  pl.BlockSpec(wr_hi.shape, const),
                  pl.BlockSpec(wr_lo.shape, const),
                  pl.BlockSpec((1, LANES), const)],
        out_specs=(pl.BlockSpec((tm, d), row),
                   pl.BlockSpec((2 * TOP_K, tm), lambda i: (0, i)),
                   pl.BlockSpec((tm, LANES), row),
                   pl.BlockSpec((8, LANES), const)),
        scratch_shapes=[pltpu.VMEM((1, LANES), F32)],
        compiler_params=_cparams(("arbitrary",)),
        name="merge_route",
    )(y_sb, y_ret, proj, proj, x2, wsb, wret, wout, g1, b1, wr_hi, wr_lo, br)


PAD_CHUNKS = tuple(1 << b for b in reversed(range(SUBLANES.bit_length() - 1,
                                                  EXPERT_BLOCK.bit_length() - 1)))


ISSUE_UNROLL = 4


def _dispatch_kernel(padstart_ref, padlen_ref, slot_ref, x_ref, xs_hbm,
                     zero_ref, sem, zsem, *, tm):
    def pad_copies(act):
        def zero_copy(off, size):
            return pltpu.make_async_copy(zero_ref.at[pl.ds(0, size)],
                                         xs_hbm.at[pl.ds(off, size)], zsem)

        def per_expert(e, c):
            n = padlen_ref[e]
            off = padstart_ref[e]
            for r in range(SUBLANES - 1):
                @pl.when(r < (n & (SUBLANES - 1)))
                def _():
                    act(zero_copy(off + r, 1))
            off = pl.multiple_of(off + (n & (SUBLANES - 1)), SUBLANES)
            for size in PAD_CHUNKS:
                @pl.when((n & size) != 0)
                def _():
                    act(zero_copy(off, size))
                off = pl.multiple_of(off + (n & size), SUBLANES)
            return c

        lax.fori_loop(0, N_EXPERTS, per_expert, 0)

        def tail_chunk(j, c):
            off = pl.multiple_of(padstart_ref[N_EXPERTS] + j * PAD_CHUNKS[0], PAD_CHUNKS[0])
            act(zero_copy(off, PAD_CHUNKS[0]))
            return c

        lax.fori_loop(0, padlen_ref[N_EXPERTS] // PAD_CHUNKS[0], tail_chunk, 0)

    @pl.when(pl.program_id(0) == 0)
    def _():
        zero_ref[...] = jnp.zeros_like(zero_ref)
        pad_copies(lambda cp: cp.start())
        pad_copies(lambda cp: cp.wait())

    def issue(t, c):
        for k in range(TOP_K):
            pltpu.make_async_copy(x_ref.at[pl.ds(t, 1)],
                                  xs_hbm.at[pl.ds(slot_ref[k, t], 1)], sem).start()
        return c

    lax.fori_loop(0, tm, issue, 0, unroll=ISSUE_UNROLL)
    for _ in range(TOP_K):
        pltpu.make_async_copy(x_ref, xs_hbm.at[pl.ds(0, tm)], sem).wait()


def _dispatch(pad_start, pad_len, slots, x1, n_rows, tm):
    n, d = x1.shape
    grid_spec = pltpu.PrefetchScalarGridSpec(
        num_scalar_prefetch=2,
        grid=(n // tm,),
        in_specs=[pl.BlockSpec((TOP_K, tm), lambda i, *_: (0, i), memory_space=pltpu.SMEM),
                  pl.BlockSpec((tm, d), lambda i, *_: (i, 0))],
        out_specs=pl.BlockSpec(memory_space=pl.ANY),
        scratch_shapes=[pltpu.VMEM((PAD_CHUNKS[0], d), x1.dtype),
                        pltpu.SemaphoreType.DMA, pltpu.SemaphoreType.DMA])
    return pl.pallas_call(
        functools.partial(_dispatch_kernel, tm=tm),
        out_shape=jax.ShapeDtypeStruct((n_rows, d), x1.dtype),
        grid_spec=grid_spec,
        compiler_params=_cparams(("arbitrary",)),
        name="dispatch",
    )(pad_start, pad_len, slots, x1)


def _expert_kernel(be_ref, nb_ref, xs_ref, wgu_ref, bgu_ref, wd_ref, bd_ref, o_ref):
    del be_ref
    i = pl.program_id(0)

    @pl.when(i < nb_ref[0])
    def _():
        hgu = jnp.dot(xs_ref[...].astype(BF16), wgu_ref[...].astype(BF16),
                      preferred_element_type=F32) + bgu_ref[...]
        d_ff = hgu.shape[1] // 2
        gate = jnp.minimum(hgu[:, :d_ff], SWIGLU_LIMIT)
        up = jnp.clip(hgu[:, d_ff:], -SWIGLU_LIMIT, SWIGLU_LIMIT)
        glu = gate * jax.nn.sigmoid(gate * SWIGLU_ALPHA)
        act = ((up + 1.0) * glu).astype(BF16)
        o_ref[...] = jnp.dot(act, wd_ref[...].astype(BF16),
                             preferred_element_type=F32) + bd_ref[...]

    @pl.when(i >= nb_ref[0])
    def _():
        o_ref[...] = jnp.zeros_like(o_ref)


def _experts(block_expert, n_used, xs, wgu, bgu, wd, bd):
    n_rows, w = xs.shape
    d = wd.shape[2]
    n_blocks = n_rows // EXPERT_BLOCK
    grid_spec = pltpu.PrefetchScalarGridSpec(
        num_scalar_prefetch=2,
        grid=(n_blocks,),
        in_specs=[pl.BlockSpec((EXPERT_BLOCK, w), lambda i, be, nb: (jnp.minimum(i, nb[0] - 1), 0)),
                  pl.BlockSpec((None,) + wgu.shape[1:], lambda i, be, nb: (be[i], 0, 0)),
                  pl.BlockSpec((None,) + bgu.shape[1:], lambda i, be, nb: (be[i], 0, 0)),
                  pl.BlockSpec((None,) + wd.shape[1:], lambda i, be, nb: (be[i], 0, 0)),
                  pl.BlockSpec((None,) + bd.shape[1:], lambda i, be, nb: (be[i], 0, 0))],
        out_specs=pl.BlockSpec((EXPERT_BLOCK, d), lambda i, be, nb: (i, 0)))
    return pl.pallas_call(
        _expert_kernel,
        out_shape=jax.ShapeDtypeStruct((n_rows, d), F32),
        grid_spec=grid_spec,
        compiler_params=_cparams(("arbitrary",)),
        name="experts",
    )(block_expert, n_used, xs, wgu, bgu, wd, bd)


def _combine_kernel(slot_ref, slot_next_ref,
                    x1_ref, topw_ref, g2_ref, b2_ref, rows_hbm,
                    o_ref, buf_ref, sem, *, tc):
    i = pl.program_id(0)
    slot = i % 2

    def issue_tile(s_ref, sl):
        def issue(t, c):
            for k in range(TOP_K):
                pltpu.make_async_copy(rows_hbm.at[pl.ds(s_ref[k, t], 1)],
                                      buf_ref.at[sl, k, pl.ds(t, 1)], sem.at[sl]).start()
            return c

        lax.fori_loop(0, tc, issue, 0, unroll=ISSUE_UNROLL)

    @pl.when(i == 0)
    def _():
        issue_tile(slot_ref, 0)

    @pl.when(i + 1 < pl.num_programs(0))
    def _():
        issue_tile(slot_next_ref, 1 - slot)

    for k in range(TOP_K):
        pltpu.make_async_copy(rows_hbm.at[pl.ds(0, tc)], buf_ref.at[slot, k], sem.at[slot]).wait()

    topw = topw_ref[...]
    y = topw[:, 0:1] * buf_ref[slot, 0]
    for k in range(1, TOP_K):
        y = y + topw[:, k:k + 1] * buf_ref[slot, k]
    o_ref[...] = _layer_norm(DEEPNORM_ALPHA * x1_ref[...] + y, g2_ref[...], b2_ref[...])


def _combine(slots, x1, topw, g2, b2, rows, tc):
    n, d = x1.shape
    nt = n // tc
    return pl.pallas_call(
        functools.partial(_combine_kernel, tc=tc),
        out_shape=jax.ShapeDtypeStruct((n, d), F32),
        grid=(nt,),
        in_specs=[pl.BlockSpec((TOP_K, tc), lambda i: (0, i), memory_space=pltpu.SMEM),
                  pl.BlockSpec((TOP_K, tc), lambda i: (0, jnp.minimum(i + 1, nt - 1)),
                               memory_space=pltpu.SMEM),
                  pl.BlockSpec((tc, d), lambda i: (i, 0)),
                  pl.BlockSpec((tc, LANES), lambda i: (i, 0)),
                  pl.BlockSpec((1, d), lambda i: (0, 0)),
                  pl.BlockSpec((1, d), lambda i: (0, 0)),
                  pl.BlockSpec(memory_space=pl.ANY)],
        out_specs=pl.BlockSpec((tc, d), lambda i: (i, 0)),
        scratch_shapes=[pltpu.VMEM((2, TOP_K, tc, d), F32), pltpu.SemaphoreType.DMA((2,))],
        compiler_params=_cparams(("arbitrary",)),
        name="combine",
    )(slots, slots, x1, topw, g2, b2, rows)


def _tile(n, pref):
    t = min(n, pref)
    assert n % t == 0, (n, t)
    return t


def kernel(x, w_in, w_branch_sb, w_branch_ret, w_out, ln1_g, ln1_b, w_router, b_router,
           w_gate_up, b_gate_up, w_down, b_down, ln2_g, ln2_b):
    b, s, d = x.shape
    n = b * s
    sb_w = SB_HEADS * SB_HEAD_DIM
    qk_w = RET_HEADS * RET_QK_DIM
    v_w = RET_HEADS * RET_V_DIM
    sizes = (sb_w, sb_w, sb_w, qk_w, qk_w, v_w, v_w, d, d)
    offs = [int(o) for o in np.cumsum((0,) + sizes)]
    assert w_in.shape[0] == DEPTH and w_in.shape[2] == offs[-1]
    assert d == sb_w and d % (2 * LANES) == 0

    half = RET_QK_DIM // 2
    inv_freq = ROPE_BASE ** (-jnp.arange(half, dtype=F32) / half)
    ang = jnp.arange(s, dtype=F32)[:, None] * inv_freq[None, :]
    cos, sin = jnp.cos(ang), jnp.sin(ang)

    n_assign = n * TOP_K
    n_rows = -(-n_assign // EXPERT_BLOCK) * EXPERT_BLOCK + N_EXPERTS * EXPERT_BLOCK
    n_blocks = n_rows // EXPERT_BLOCK

    col_scale = jnp.ones((1, offs[-1]), F32).at[:, :sb_w].set(SB_HEAD_DIM ** -0.5 * LOG2_E)

    xcur = x.reshape(n, d)
    for l in range(DEPTH):
        proj = _in_proj(xcur, w_in[l].astype(BF16), col_scale, _tile(n, 1024), offs[-1] // 4)
        proj3 = proj.reshape(b, s, offs[-1])
        y_sb = _sb_attention(proj3, _tile(s, 128), SB_HEADS)
        y_ret = _retention(proj3, cos, sin, _tile(s, 512), offs[3], offs[4], offs[5], offs[6])

        wr = jnp.zeros((d, LANES), F32).at[:, :N_EXPERTS].set(w_router[l])
        br = jnp.zeros((1, LANES), F32).at[0, :N_EXPERTS].set(b_router[l])
        wr_hi = wr.astype(BF16)
        wr_lo = (wr - wr_hi.astype(F32)).astype(BF16)
        x1, route, topw, cnt = _merge_route(
            y_sb.reshape(n, sb_w), y_ret.reshape(n, v_w), proj, xcur,
            w_branch_sb[l].astype(BF16), w_branch_ret[l].astype(BF16), w_out[l].astype(BF16),
            ln1_g[l].reshape(1, d), ln1_b[l].reshape(1, d), wr_hi, wr_lo, br,
            _tile(n, 512), offs[7], offs[8])

        counts = cnt[0, :N_EXPERTS]
        padded = (counts + EXPERT_BLOCK - 1) // EXPERT_BLOCK * EXPERT_BLOCK
        padded_ends = jnp.cumsum(padded)
        starts = (padded_ends - padded).astype(jnp.int32)
        n_used = (padded_ends[-1:] // EXPERT_BLOCK).astype(jnp.int32)
        block_rows = jnp.arange(n_blocks, dtype=jnp.int32) * EXPERT_BLOCK
        block_expert = jnp.minimum(
            jnp.sum((padded_ends[None, :] <= block_rows[:, None]).astype(jnp.int32), axis=1),
            N_EXPERTS - 1).astype(jnp.int32)
        pad_start = jnp.concatenate([starts + counts, padded_ends[-1:]]).astype(jnp.int32)
        pad_len = jnp.concatenate([padded - counts, n_rows - padded_ends[-1:]]).astype(jnp.int32)

        expert_ids = jnp.arange(N_EXPERTS, dtype=jnp.int32)[:, None, None]
        slots = route[TOP_K:] + jnp.sum(
            jnp.where(route[None, :TOP_K] == expert_ids, starts[:, None, None], 0), axis=0)

        xs = _dispatch(pad_start, pad_len, slots, x1, n_rows, _tile(n, 512))
        rows = _experts(block_expert, n_used, xs,
                        w_gate_up[l], b_gate_up[l][:, None, :],
                        w_down[l], b_down[l][:, None, :])
        xcur = _combine(slots, x1, topw, ln2_g[l].reshape(1, d),
                        ln2_b[l].reshape(1, d), rows, _tile(n, 256))
    return xcur.reshape(b, s, d)
```

```python
import functools

import numpy as np
import jax
import jax.numpy as jnp
from jax import lax
from jax.experimental import pallas as pl
from jax.experimental.pallas import tpu as pltpu

F32 = jnp.float32
BF16 = jnp.bfloat16

SB_HEADS = 16
SB_HEAD_DIM = 64
RET_HEADS = 4
RET_QK_DIM = 256
RET_V_DIM = 512
RET_CHUNK = 64
ROPE_BASE = 10000.0
N_EXPERTS = 32
TOP_K = 4
EXPERT_BLOCK = 512
SWIGLU_LIMIT = 7.0
SWIGLU_ALPHA = 1.702
LN_EPS = 1e-5
DEPTH = 1
DEEPNORM_ALPHA = (2.0 * DEPTH) ** 0.25

LANES = 128
SUBLANES = 8
VMEM_LIMIT = 56 * 1024 * 1024

EXP2_UNDERFLOW_BITS = 151.0
LOG2_E = 1.4426950408889634
MASKED_SCORE = -1e30


def _cparams(sem):
    return pltpu.CompilerParams(dimension_semantics=sem, vmem_limit_bytes=VMEM_LIMIT)


def _proj_kernel(x_ref, w_ref, cs_ref, o_ref, xb_ref):
    @pl.when(pl.program_id(1) == 0)
    def _():
        xb_ref[...] = x_ref[...].astype(BF16)

    acc = jnp.dot(xb_ref[...], w_ref[...], preferred_element_type=F32)
    o_ref[...] = (acc * cs_ref[...]).astype(o_ref.dtype)


def _in_proj(x2, w_bf, col_scale, tm, tn):
    n, d = x2.shape
    width = w_bf.shape[1]
    return pl.pallas_call(
        _proj_kernel,
        out_shape=jax.ShapeDtypeStruct((n, width), BF16),
        grid=(n // tm, width // tn),
        in_specs=[pl.BlockSpec((tm, d), lambda i, j: (i, 0)),
                  pl.BlockSpec((d, tn), lambda i, j: (0, j)),
                  pl.BlockSpec((1, tn), lambda i, j: (0, j))],
        out_specs=pl.BlockSpec((tm, tn), lambda i, j: (i, j)),
        scratch_shapes=[pltpu.VMEM((tm, d), BF16)],
        compiler_params=_cparams(("parallel", "arbitrary")),
        name="in_proj",
    )(x2, w_bf, col_scale)


def _sb_kernel(q_ref, k_ref, v_ref, o_ref, acc_ref, carry_ref, *, tq, n_heads):
    qi = pl.program_id(2)
    n_groups = n_heads // 2
    lane = lax.broadcasted_iota(jnp.int32, (tq, LANES), 1)
    low = lane < SB_HEAD_DIM
    q = q_ref[...]
    zero = jnp.zeros((tq, LANES), BF16)
    q_pairs = []
    for g in range(n_groups):
        qg = q[:, g * LANES:(g + 1) * LANES]
        q_pairs.append(jnp.concatenate([jnp.where(low, qg, zero), jnp.where(low, zero, qg)], axis=0))
    row = lax.broadcasted_iota(jnp.int32, (tq, tq), 0)
    col = lax.broadcasted_iota(jnp.int32, (tq, tq), 1)
    suffix = (row >= col).astype(BF16)
    rows_all = lax.broadcasted_iota(jnp.int32, (n_heads * tq, tq), 0)
    cols_all = lax.broadcasted_iota(jnp.int32, (n_heads * tq, tq), 1)
    strict = cols_all < (rows_all & (tq - 1))

    acc_ref[...] = jnp.zeros_like(acc_ref)
    carry_ref[...] = jnp.zeros_like(carry_ref)

    def block(kb, diag):
        start = pl.multiple_of(kb * tq, tq)
        z = jnp.concatenate(
            [lax.dot_general(q_pairs[g], k_ref[pl.ds(start, tq), pl.ds(g * LANES, LANES)],
                             (((1,), (1,)), ((), ())), preferred_element_type=F32)
             for g in range(n_groups)], axis=0)
        if diag:
            z = jnp.where(strict, z, MASKED_SCORE)
        sp = jnp.maximum(z, 0.0) + jnp.log2(1.0 + jnp.exp2(-jnp.abs(z)))
        cum = jnp.dot(sp.astype(BF16), suffix, preferred_element_type=F32)
        carry = carry_ref[...]
        a = jnp.exp2(z - cum - carry).astype(BF16)
        pv = jnp.concatenate(
            [jnp.dot(a[g * 2 * tq:(g + 1) * 2 * tq], v_ref[pl.ds(start, tq), pl.ds(g * LANES, LANES)],
                     preferred_element_type=F32)
             for g in range(n_groups)], axis=0)
        acc_ref[...] += pv
        carry_ref[...] = carry + cum[:, 0:1]

    def alive():
        return jnp.min(carry_ref[...]) <= EXP2_UNDERFLOW_BITS

    block(qi, True)

    def cond(c):
        kb, live = c
        return jnp.logical_and(kb >= 0, live)

    def body(c):
        kb, _ = c
        block(kb, False)
        return kb - 1, alive()

    lax.while_loop(cond, body, (qi - 1, alive()))
    for g in range(n_groups):
        o_ref[:, g * LANES:(g + 1) * LANES] = jnp.where(
            low, acc_ref[pl.ds(2 * g * tq, tq), :], acc_ref[pl.ds((2 * g + 1) * tq, tq), :]
        ).astype(o_ref.dtype)


def _sb_attention(proj3, tq, n_heads):
    b, s, _ = proj3.shape
    width = n_heads * SB_HEAD_DIM
    groups = SB_HEADS // n_heads
    resident = pl.Buffered(1)
    return pl.pallas_call(
        functools.partial(_sb_kernel, tq=tq, n_heads=n_heads),
        out_shape=jax.ShapeDtypeStruct((b, s, SB_HEADS * SB_HEAD_DIM), BF16),
        grid=(b, groups, s // tq),
        in_specs=[pl.BlockSpec((None, tq, width), lambda bi, g, i: (bi, i, g)),
                  pl.BlockSpec((None, s, width), lambda bi, g, i: (bi, 0, groups + g),
                               pipeline_mode=resident),
                  pl.BlockSpec((None, s, width), lambda bi, g, i: (bi, 0, 2 * groups + g),
                               pipeline_mode=resident)],
        out_specs=pl.BlockSpec((None, tq, width), lambda bi, g, i: (bi, i, g)),
        scratch_shapes=[pltpu.VMEM((n_heads * tq, LANES), F32), pltpu.VMEM((n_heads * tq, 1), F32)],
        compiler_params=_cparams(("parallel", "parallel", "arbitrary")),
        name="sb_attn",
    )(proj3, proj3, proj3)


def _ret_kernel(lg_ref, q_ref, k_ref, v_ref, g_ref, cos_ref, sin_ref, o_ref,
                state_ref, mask_ref, qdec_ref, kdec_ref, *, tb):
    h = pl.program_id(1)
    si = pl.program_id(2)
    lg = lg_ref[h]
    half = RET_QK_DIM // 2

    @pl.when(si == 0)
    def _():
        state_ref[...] = jnp.zeros_like(state_ref)
        i = lax.broadcasted_iota(jnp.int32, (tb, tb), 0)
        j = lax.broadcasted_iota(jnp.int32, (tb, tb), 1)
        dist = jnp.abs(i - j).astype(F32)
        visible = (j // RET_CHUNK) <= (i // RET_CHUNK)
        mask_ref[...] = jnp.where(visible, jnp.exp(lg * dist), 0.0)
        r = lax.broadcasted_iota(jnp.int32, (tb, 1), 0).astype(F32)
        qdec_ref[...] = jnp.exp(lg * (r + 1.0))
        kdec_ref[...] = jnp.exp(lg * (tb - 1.0 - r))

    cos = cos_ref[...]
    sin = sin_ref[...]

    def rot(t):
        t1 = t[:, :half]
        t2 = t[:, half:]
        return jnp.concatenate([t1 * cos - t2 * sin, t1 * sin + t2 * cos], axis=-1)

    qr = rot(q_ref[...].astype(F32)) * (RET_QK_DIM ** -0.5)
    kr = rot(k_ref[...].astype(F32))
    v = v_ref[...]
    scores = lax.dot_general(qr.astype(BF16), kr.astype(BF16), (((1,), (1,)), ((), ())),
                             preferred_element_type=F32) * mask_ref[...]
    state = state_ref[...]
    y = (jnp.dot(scores.astype(BF16), v, preferred_element_type=F32)
         + jnp.dot((qr * qdec_ref[...]).astype(BF16), state.astype(BF16),
                   preferred_element_type=F32))
    kd = (kr * kdec_ref[...]).astype(BF16)
    state_ref[...] = state * jnp.exp(lg * tb) + lax.dot_general(
        kd, v, (((0,), (0,)), ((), ())), preferred_element_type=F32)

    mu = jnp.mean(y, axis=-1, keepdims=True)
    yc = y - mu
    var = jnp.mean(yc * yc, axis=-1, keepdims=True)
    yn = yc * lax.rsqrt(var + LN_EPS)
    g = g_ref[...].astype(F32)
    o_ref[...] = (g * jax.nn.sigmoid(g) * yn).astype(o_ref.dtype)


def _retention(proj3, cos, sin, tb, off_q, off_k, off_v, off_g):
    b, s, _ = proj3.shape
    log_gamma = jnp.asarray(np.log(1.0 - 2.0 ** (-5.0 - np.arange(RET_HEADS))), F32)
    qb, kb = off_q // RET_QK_DIM, off_k // RET_QK_DIM
    vb, gb = off_v // RET_V_DIM, off_g // RET_V_DIM
    half = RET_QK_DIM // 2
    grid_spec = pltpu.PrefetchScalarGridSpec(
        num_scalar_prefetch=1,
        grid=(b, RET_HEADS, s // tb),
        in_specs=[pl.BlockSpec((None, tb, RET_QK_DIM), lambda bi, h, i, lg: (bi, i, qb + h)),
                  pl.BlockSpec((None, tb, RET_QK_DIM), lambda bi, h, i, lg: (bi, i, kb + h)),
                  pl.BlockSpec((None, tb, RET_V_DIM), lambda bi, h, i, lg: (bi, i, vb + h)),
                  pl.BlockSpec((None, tb, RET_V_DIM), lambda bi, h, i, lg: (bi, i, gb + h)),
                  pl.BlockSpec((tb, half), lambda bi, h, i, lg: (i, 0)),
                  pl.BlockSpec((tb, half), lambda bi, h, i, lg: (i, 0))],
        out_specs=pl.BlockSpec((None, tb, RET_V_DIM), lambda bi, h, i, lg: (bi, i, h)),
        scratch_shapes=[pltpu.VMEM((RET_QK_DIM, RET_V_DIM), F32),
                        pltpu.VMEM((tb, tb), F32),
                        pltpu.VMEM((tb, 1), F32),
                        pltpu.VMEM((tb, 1), F32)])
    return pl.pallas_call(
        functools.partial(_ret_kernel, tb=tb),
        out_shape=jax.ShapeDtypeStruct((b, s, RET_HEADS * RET_V_DIM), BF16),
        grid_spec=grid_spec,
        compiler_params=_cparams(("parallel", "parallel", "arbitrary")),
        name="retention",
    )(log_gamma, proj3, proj3, proj3, proj3, cos, sin)


def _layer_norm(h, g, b):
    mu = jnp.mean(h, axis=-1, keepdims=True)
    hc = h - mu
    var = jnp.mean(hc * hc, axis=-1, keepdims=True)
    return hc * lax.rsqrt(var + LN_EPS) * g + b


def _merge_kernel(ysb_ref, yret_ref, gsb_ref, gret_ref, x_ref, wsb_ref, wret_ref, wout_ref,
                  g1_ref, b1_ref, wrh_ref, wrl_ref, br_ref,
                  x1_ref, route_ref, topw_ref, cnt_ref, run_ref, *, tm):
    i = pl.program_id(0)

    @pl.when(i == 0)
    def _():
        run_ref[...] = jnp.zeros_like(run_ref)

    bsb = jnp.dot(ysb_ref[...], wsb_ref[...], preferred_element_type=F32)
    bret = jnp.dot(yret_ref[...], wret_ref[...], preferred_element_type=F32)
    merged = (jax.nn.sigmoid(gsb_ref[...].astype(F32)) * bsb
              + jax.nn.sigmoid(gret_ref[...].astype(F32)) * bret)
    mix = jnp.dot(merged.astype(BF16), wout_ref[...], preferred_element_type=F32)
    x1 = _layer_norm(DEEPNORM_ALPHA * x_ref[...] + mix, g1_ref[...], b1_ref[...])
    x1_ref[...] = x1

    x_hi = x1.astype(BF16)
    x_lo = (x1 - x_hi.astype(F32)).astype(BF16)
    logits = (jnp.dot(x_hi, wrh_ref[...], preferred_element_type=F32)
              + jnp.dot(x_hi, wrl_ref[...], preferred_element_type=F32)
              + jnp.dot(x_lo, wrh_ref[...], preferred_element_type=F32)) + br_ref[...]
    lane = lax.broadcasted_iota(jnp.int32, (tm, LANES), 1)
    lane_f = lane.astype(F32)
    neg = jnp.asarray(-jnp.inf, F32)
    work = jnp.where(lane < N_EXPERTS, logits, neg)
    vals, idxs = [], []
    sel = jnp.zeros((tm, LANES), F32)
    for _ in range(TOP_K):
        m = jnp.max(work, axis=-1, keepdims=True)
        idx = jnp.min(jnp.where(work == m, lane_f, float(LANES)), axis=-1, keepdims=True)
        hit = lane_f == idx
        sel = jnp.where(hit, 1.0, sel)
        work = jnp.where(hit, neg, work)
        vals.append(m)
        idxs.append(idx)
    exps = [jnp.exp(v - vals[0]) for v in vals]
    denom = exps[0] + exps[1] + exps[2] + exps[3]
    weights = [e / denom for e in exps]

    r = lax.broadcasted_iota(jnp.int32, (tm, tm), 0)
    c = lax.broadcasted_iota(jnp.int32, (tm, tm), 1)
    before = (c < r).astype(BF16)
    rank_mat = jnp.dot(before, sel.astype(BF16), preferred_element_type=F32) + run_ref[...]
    route = jnp.zeros((tm, LANES), F32)
    topw = jnp.zeros((tm, LANES), F32)
    for k in range(TOP_K):
        rank_k = jnp.sum(jnp.where(lane_f == idxs[k], rank_mat, 0.0), axis=-1, keepdims=True)
        route = jnp.where(lane == k, idxs[k], route)
        route = jnp.where(lane == TOP_K + k, rank_k, route)
        topw = jnp.where(lane == k, weights[k], topw)
    route_ref[...] = route.T[:2 * TOP_K, :].astype(jnp.int32)
    topw_ref[...] = topw
    run = run_ref[...] + jnp.sum(sel, axis=0, keepdims=True)
    run_ref[...] = run
    cnt_ref[...] = jnp.broadcast_to(run, cnt_ref.shape).astype(jnp.int32)


def _merge_route(y_sb, y_ret, proj, x2, wsb, wret, wout, g1, b1, wr_hi, wr_lo, br, tm,
                 off_gsb, off_gret):
    n, d = x2.shape
    const = lambda i: (0, 0)
    row = lambda i: (i, 0)
    return pl.pallas_call(
        functools.partial(_merge_kernel, tm=tm),
        out_shape=(jax.ShapeDtypeStruct((n, d), F32),
                   jax.ShapeDtypeStruct((2 * TOP_K, n), jnp.int32),
                   jax.ShapeDtypeStruct((n, LANES), F32),
                   jax.ShapeDtypeStruct((8, LANES), jnp.int32)),
        grid=(n // tm,),
        in_specs=[pl.BlockSpec((tm, y_sb.shape[1]), row),
                  pl.BlockSpec((tm, y_ret.shape[1]), row),
                  pl.BlockSpec((tm, d), lambda i: (i, off_gsb // d)),
                  pl.BlockSpec((tm, d), lambda i: (i, off_gret // d)),
                  pl.BlockSpec((tm, d), row),
                  pl.BlockSpec(wsb.shape, const),
                  pl.BlockSpec(wret.shape, const),
                  pl.BlockSpec(wout.shape, const),
                  pl.BlockSpec((1, d), const),
                  pl.BlockSpec((1, d), const),
                  pl.BlockSpec(wr_hi.shape, const),
                  pl.BlockSpec(wr_lo.shape, const),
                  pl.BlockSpec((1, LANES), const)],
        out_specs=(pl.BlockSpec((tm, d), row),
                   pl.BlockSpec((2 * TOP_K, tm), lambda i: (0, i)),
                   pl.BlockSpec((tm, LANES), row),
                   pl.BlockSpec((8, LANES), const)),
        scratch_shapes=[pltpu.VMEM((1, LANES), F32)],
        compiler_params=_cparams(("arbitrary",)),
        name="merge_route",
    )(y_sb, y_ret, proj, proj, x2, wsb, wret, wout, g1, b1, wr_hi, wr_lo, br)


PAD_CHUNKS = tuple(1 << b for b in reversed(range(SUBLANES.bit_length() - 1,
                                                  EXPERT_BLOCK.bit_length() - 1)))


ISSUE_UNROLL = 4


def _dispatch_kernel(padstart_ref, padlen_ref, slot_ref, x_ref, xs_hbm,
                     zero_ref, sem, zsem, *, tm):
    def pad_copies(act):
        def zero_copy(off, size):
            return pltpu.make_async_copy(zero_ref.at[pl.ds(0, size)],
                                         xs_hbm.at[pl.ds(off, size)], zsem)

        def per_expert(e, c):
            n = padlen_ref[e]
            off = padstart_ref[e]
            for r in range(SUBLANES - 1):
                @pl.when(r < (n & (SUBLANES - 1)))
                def _():
                    act(zero_copy(off + r, 1))
            off = pl.multiple_of(off + (n & (SUBLANES - 1)), SUBLANES)
            for size in PAD_CHUNKS:
                @pl.when((n & size) != 0)
                def _():
                    act(zero_copy(off, size))
                off = pl.multiple_of(off + (n & size), SUBLANES)
            return c

        lax.fori_loop(0, N_EXPERTS, per_expert, 0)

        def tail_chunk(j, c):
            off = pl.multiple_of(padstart_ref[N_EXPERTS] + j * PAD_CHUNKS[0], PAD_CHUNKS[0])
            act(zero_copy(off, PAD_CHUNKS[0]))
            return c

        lax.fori_loop(0, padlen_ref[N_EXPERTS] // PAD_CHUNKS[0], tail_chunk, 0)

    @pl.when(pl.program_id(0) == 0)
    def _():
        zero_ref[...] = jnp.zeros_like(zero_ref)
        pad_copies(lambda cp: cp.start())
        pad_copies(lambda cp: cp.wait())

    def issue(t, c):
        for k in range(TOP_K):
            pltpu.make_async_copy(x_ref.at[pl.ds(t, 1)],
                                  xs_hbm.at[pl.ds(slot_ref[k, t], 1)], sem).start()
        return c

    lax.fori_loop(0, tm, issue, 0, unroll=ISSUE_UNROLL)
    for _ in range(TOP_K):
        pltpu.make_async_copy(x_ref, xs_hbm.at[pl.ds(0, tm)], sem).wait()


def _dispatch(pad_start, pad_len, slots, x1, n_rows, tm):
    n, d = x1.shape
    grid_spec = pltpu.PrefetchScalarGridSpec(
        num_scalar_prefetch=2,
        grid=(n // tm,),
        in_specs=[pl.BlockSpec((TOP_K, tm), lambda i, *_: (0, i), memory_space=pltpu.SMEM),
                  pl.BlockSpec((tm, d), lambda i, *_: (i, 0))],
        out_specs=pl.BlockSpec(memory_space=pl.ANY),
        scratch_shapes=[pltpu.VMEM((PAD_CHUNKS[0], d), x1.dtype),
                        pltpu.SemaphoreType.DMA, pltpu.SemaphoreType.DMA])
    return pl.pallas_call(
        functools.partial(_dispatch_kernel, tm=tm),
        out_shape=jax.ShapeDtypeStruct((n_rows, d), x1.dtype),
        grid_spec=grid_spec,
        compiler_params=_cparams(("arbitrary",)),
        name="dispatch",
    )(pad_start, pad_len, slots, x1)


def _expert_kernel(be_ref, nb_ref, xs_ref, wgu_ref, bgu_ref, wd_ref, bd_ref, o_ref):
    del be_ref
    i = pl.program_id(0)

    @pl.when(i < nb_ref[0])
    def _():
        hgu = jnp.dot(xs_ref[...].astype(BF16), wgu_ref[...].astype(BF16),
                      preferred_element_type=F32) + bgu_ref[...]
        d_ff = hgu.shape[1] // 2
        gate = jnp.minimum(hgu[:, :d_ff], SWIGLU_LIMIT)
        up = jnp.clip(hgu[:, d_ff:], -SWIGLU_LIMIT, SWIGLU_LIMIT)
        glu = gate * jax.nn.sigmoid(gate * SWIGLU_ALPHA)
        act = ((up + 1.0) * glu).astype(BF16)
        o_ref[...] = jnp.dot(act, wd_ref[...].astype(BF16),
                             preferred_element_type=F32) + bd_ref[...]

    @pl.when(i >= nb_ref[0])
    def _():
        o_ref[...] = jnp.zeros_like(o_ref)


def _experts(block_expert, n_used, xs, wgu, bgu, wd, bd):
    n_rows, w = xs.shape
    d = wd.shape[2]
    n_blocks = n_rows // EXPERT_BLOCK
    grid_spec = pltpu.PrefetchScalarGridSpec(
        num_scalar_prefetch=2,
        grid=(n_blocks,),
        in_specs=[pl.BlockSpec((EXPERT_BLOCK, w), lambda i, be, nb: (jnp.minimum(i, nb[0] - 1), 0)),
                  pl.BlockSpec((None,) + wgu.shape[1:], lambda i, be, nb: (be[i], 0, 0)),
                  pl.BlockSpec((None,) + bgu.shape[1:], lambda i, be, nb: (be[i], 0, 0)),
                  pl.BlockSpec((None,) + wd.shape[1:], lambda i, be, nb: (be[i], 0, 0)),
                  pl.BlockSpec((None,) + bd.shape[1:], lambda i, be, nb: (be[i], 0, 0))],
        out_specs=pl.BlockSpec((EXPERT_BLOCK, d), lambda i, be, nb: (i, 0)))
    return pl.pallas_call(
        _expert_kernel,
        out_shape=jax.ShapeDtypeStruct((n_rows, d), F32),
        grid_spec=grid_spec,
        compiler_params=_cparams(("arbitrary",)),
        name="experts",
    )(block_expert, n_used, xs, wgu, bgu, wd, bd)


def _combine_kernel(slot_ref, slot_next_ref,
                    x1_ref, topw_ref, g2_ref, b2_ref, rows_hbm,
                    o_ref, buf_ref, sem, *, tc):
    i = pl.program_id(0)

    def issue_tile(s_ref, sl):
        def issue(t, c):
            for k in range(TOP_K):
                pltpu.make_async_copy(rows_hbm.at[pl.ds(s_ref[k, t], 1)],
                                      buf_ref.at[sl, k, pl.ds(t, 1)], sem.at[sl]).start()
            return c

        lax.fori_loop(0, tc, issue, 0, unroll=ISSUE_UNROLL)

    @pl.when(i == 0)
    def _():
        issue_tile(slot_ref, 0)

    def step(slot):
        @pl.when(i + 1 < pl.num_programs(0))
        def _():
            issue_tile(slot_next_ref, 1 - slot)

        for k in range(TOP_K):
            pltpu.make_async_copy(rows_hbm.at[pl.ds(0, tc)], buf_ref.at[slot, k],
                                  sem.at[slot]).wait()

        topw = topw_ref[...]
        y = topw[:, 0:1] * buf_ref[slot, 0]
        for k in range(1, TOP_K):
            y = y + topw[:, k:k + 1] * buf_ref[slot, k]
        o_ref[...] = _layer_norm(DEEPNORM_ALPHA * x1_ref[...] + y, g2_ref[...], b2_ref[...])

    for parity in range(2):
        @pl.when(i % 2 == parity)
        def _():
            step(parity)


def _combine(slots, x1, topw, g2, b2, rows, tc):
    n, d = x1.shape
    nt = n // tc
    return pl.pallas_call(
        functools.partial(_combine_kernel, tc=tc),
        out_shape=jax.ShapeDtypeStruct((n, d), F32),
        grid=(nt,),
        in_specs=[pl.BlockSpec((TOP_K, tc), lambda i: (0, i), memory_space=pltpu.SMEM),
                  pl.BlockSpec((TOP_K, tc), lambda i: (0, jnp.minimum(i + 1, nt - 1)),
                               memory_space=pltpu.SMEM),
                  pl.BlockSpec((tc, d), lambda i: (i, 0)),
                  pl.BlockSpec((tc, LANES), lambda i: (i, 0)),
                  pl.BlockSpec((1, d), lambda i: (0, 0)),
                  pl.BlockSpec((1, d), lambda i: (0, 0)),
                  pl.BlockSpec(memory_space=pl.ANY)],
        out_specs=pl.BlockSpec((tc, d), lambda i: (i, 0)),
        scratch_shapes=[pltpu.VMEM((2, TOP_K, tc, d), F32), pltpu.SemaphoreType.DMA((2,))],
        compiler_params=_cparams(("arbitrary",)),
        name="combine",
    )(slots, slots, x1, topw, g2, b2, rows)


def _tile(n, pref):
    t = min(n, pref)
    assert n % t == 0, (n, t)
    return t


def kernel(x, w_in, w_branch_sb, w_branch_ret, w_out, ln1_g, ln1_b, w_router, b_router,
           w_gate_up, b_gate_up, w_down, b_down, ln2_g, ln2_b):
    b, s, d = x.shape
    n = b * s
    sb_w = SB_HEADS * SB_HEAD_DIM
    qk_w = RET_HEADS * RET_QK_DIM
    v_w = RET_HEADS * RET_V_DIM
    sizes = (sb_w, sb_w, sb_w, qk_w, qk_w, v_w, v_w, d, d)
    offs = [int(o) for o in np.cumsum((0,) + sizes)]
    assert w_in.shape[0] == DEPTH and w_in.shape[2] == offs[-1]
    assert d == sb_w and d % (2 * LANES) == 0

    half = RET_QK_DIM // 2
    inv_freq = ROPE_BASE ** (-jnp.arange(half, dtype=F32) / half)
    ang = jnp.arange(s, dtype=F32)[:, None] * inv_freq[None, :]
    cos, sin = jnp.cos(ang), jnp.sin(ang)

    n_assign = n * TOP_K
    n_rows = -(-n_assign // EXPERT_BLOCK) * EXPERT_BLOCK + N_EXPERTS * EXPERT_BLOCK
    n_blocks = n_rows // EXPERT_BLOCK

    col_scale = jnp.ones((1, offs[-1]), F32).at[:, :sb_w].set(SB_HEAD_DIM ** -0.5 * LOG2_E)

    xcur = x.reshape(n, d)
    for l in range(DEPTH):
        proj = _in_proj(xcur, w_in[l].astype(BF16), col_scale, _tile(n, 1024), offs[-1] // 4)
        proj3 = proj.reshape(b, s, offs[-1])
        y_sb = _sb_attention(proj3, _tile(s, 128), SB_HEADS)
        y_ret = _retention(proj3, cos, sin, _tile(s, 512), offs[3], offs[4], offs[5], offs[6])

        wr = jnp.zeros((d, LANES), F32).at[:, :N_EXPERTS].set(w_router[l])
        br = jnp.zeros((1, LANES), F32).at[0, :N_EXPERTS].set(b_router[l])
        wr_hi = wr.astype(BF16)
        wr_lo = (wr - wr_hi.astype(F32)).astype(BF16)
        x1, route, topw, cnt = _merge_route(
            y_sb.reshape(n, sb_w), y_ret.reshape(n, v_w), proj, xcur,
            w_branch_sb[l].astype(BF16), w_branch_ret[l].astype(BF16), w_out[l].astype(BF16),
            ln1_g[l].reshape(1, d), ln1_b[l].reshape(1, d), wr_hi, wr_lo, br,
            _tile(n, 512), offs[7], offs[8])

        counts = cnt[0, :N_EXPERTS]
        padded = (counts + EXPERT_BLOCK - 1) // EXPERT_BLOCK * EXPERT_BLOCK
        padded_ends = jnp.cumsum(padded)
        starts = (padded_ends - padded).astype(jnp.int32)
        n_used = (padded_ends[-1:] // EXPERT_BLOCK).astype(jnp.int32)
        block_rows = jnp.arange(n_blocks, dtype=jnp.int32) * EXPERT_BLOCK
        block_expert = jnp.minimum(
            jnp.sum((padded_ends[None, :] <= block_rows[:, None]).astype(jnp.int32), axis=1),
            N_EXPERTS - 1).astype(jnp.int32)
        pad_start = jnp.concatenate([starts + counts, padded_ends[-1:]]).astype(jnp.int32)
        pad_len = jnp.concatenate([padded - counts, n_rows - padded_ends[-1:]]).astype(jnp.int32)

        expert_ids = jnp.arange(N_EXPERTS, dtype=jnp.int32)[:, None, None]
        slots = route[TOP_K:] + jnp.sum(
            jnp.where(route[None, :TOP_K] == expert_ids, starts[:, None, None], 0), axis=0)

        xs = _dispatch(pad_start, pad_len, slots, x1, n_rows, _tile(n, 512))
        rows = _experts(block_expert, n_used, xs,
                        w_gate_up[l], b_gate_up[l][:, None, :],
                        w_down[l], b_down[l][:, None, :])
        xcur = _combine(slots, x1, topw, ln2_g[l].reshape(1, d),
                        ln2_b[l].reshape(1, d), rows, _tile(n, 512))
    return xcur.reshape(b, s, d)
```

```python
import functools

import numpy as np
import jax
import jax.numpy as jnp
from jax import lax
from jax.experimental import pallas as pl
from jax.experimental.pallas import tpu as pltpu

F32 = jnp.float32
BF16 = jnp.bfloat16

SB_HEADS = 16
SB_HEAD_DIM = 64
RET_HEADS = 4
RET_QK_DIM = 256
RET_V_DIM = 512
RET_CHUNK = 64
ROPE_BASE = 10000.0
N_EXPERTS = 32
TOP_K = 4
EXPERT_BLOCK = 512
SWIGLU_LIMIT = 7.0
SWIGLU_ALPHA = 1.702
LN_EPS = 1e-5
DEPTH = 1
DEEPNORM_ALPHA = (2.0 * DEPTH) ** 0.25

LANES = 128
SUBLANES = 8
VMEM_LIMIT = 56 * 1024 * 1024

EXP2_UNDERFLOW_BITS = 151.0
LOG2_E = 1.4426950408889634
MASKED_SCORE = -1e30


def _cparams(sem):
    return pltpu.CompilerParams(dimension_semantics=sem, vmem_limit_bytes=VMEM_LIMIT)


def _proj_kernel(x_ref, w_ref, cs_ref, o_ref, xb_ref):
    @pl.when(pl.program_id(1) == 0)
    def _():
        xb_ref[...] = x_ref[...].astype(BF16)

    acc = jnp.dot(xb_ref[...], w_ref[...], preferred_element_type=F32)
    o_ref[...] = (acc * cs_ref[...]).astype(o_ref.dtype)


def _in_proj(x2, w_bf, col_scale, tm, tn):
    n, d = x2.shape
    width = w_bf.shape[1]
    return pl.pallas_call(
        _proj_kernel,
        out_shape=jax.ShapeDtypeStruct((n, width), BF16),
        grid=(n // tm, width // tn),
        in_specs=[pl.BlockSpec((tm, d), lambda i, j: (i, 0)),
                  pl.BlockSpec((d, tn), lambda i, j: (0, j)),
                  pl.BlockSpec((1, tn), lambda i, j: (0, j))],
        out_specs=pl.BlockSpec((tm, tn), lambda i, j: (i, j)),
        scratch_shapes=[pltpu.VMEM((tm, d), BF16)],
        compiler_params=_cparams(("parallel", "arbitrary")),
        name="in_proj",
    )(x2, w_bf, col_scale)


def _sb_kernel(q_ref, k_ref, v_ref, o_ref, acc_ref, carry_ref, *, tq, n_heads):
    qi = pl.program_id(2)
    n_groups = n_heads // 2
    lane = lax.broadcasted_iota(jnp.int32, (tq, LANES), 1)
    low = lane < SB_HEAD_DIM
    q = q_ref[...]
    zero = jnp.zeros((tq, LANES), BF16)
    q_pairs = []
    for g in range(n_groups):
        qg = q[:, g * LANES:(g + 1) * LANES]
        q_pairs.append(jnp.concatenate([jnp.where(low, qg, zero), jnp.where(low, zero, qg)], axis=0))
    def suffix_matrix(width):
        row = lax.broadcasted_iota(jnp.int32, (width, width), 0)
        col = lax.broadcasted_iota(jnp.int32, (width, width), 1)
        return (row >= col).astype(BF16)

    wide = 2 * tq
    suffix = {tq: suffix_matrix(tq), wide: suffix_matrix(wide)}
    rows_all = lax.broadcasted_iota(jnp.int32, (n_heads * tq, tq), 0)
    cols_all = lax.broadcasted_iota(jnp.int32, (n_heads * tq, tq), 1)
    strict = cols_all < (rows_all & (tq - 1))

    acc_ref[...] = jnp.zeros_like(acc_ref)
    carry_ref[...] = jnp.zeros_like(carry_ref)

    def block(start, width, diag):
        keys = pl.ds(pl.multiple_of(start, tq), width)
        z = jnp.concatenate(
            [lax.dot_general(q_pairs[g], k_ref[keys, pl.ds(g * LANES, LANES)],
                             (((1,), (1,)), ((), ())), preferred_element_type=F32)
             for g in range(n_groups)], axis=0)
        if diag:
            z = jnp.where(strict, z, MASKED_SCORE)
        sp = jnp.maximum(z, 0.0) + jnp.log2(1.0 + jnp.exp2(-jnp.abs(z)))
        cum = jnp.dot(sp.astype(BF16), suffix[width], preferred_element_type=F32)
        carry = carry_ref[...]
        a = jnp.exp2(z - cum - carry).astype(BF16)
        pv = jnp.concatenate(
            [jnp.dot(a[g * 2 * tq:(g + 1) * 2 * tq], v_ref[keys, pl.ds(g * LANES, LANES)],
                     preferred_element_type=F32)
             for g in range(n_groups)], axis=0)
        acc_ref[...] += pv
        carry_ref[...] = carry + cum[:, 0:1]

    def alive():
        return jnp.min(carry_ref[...]) <= EXP2_UNDERFLOW_BITS

    block(qi * tq, tq, True)

    n_wide = qi // 2

    def cond(c):
        j, live = c
        return jnp.logical_and(j < n_wide, live)

    def body(c):
        j, _ = c
        block(qi * tq - wide * (j + 1), wide, False)
        return j + 1, alive()

    _, live = lax.while_loop(cond, body, (0, alive()))

    @pl.when(jnp.logical_and(live, qi % 2 == 1))
    def _():
        block(0, tq, False)
    for g in range(n_groups):
        o_ref[:, g * LANES:(g + 1) * LANES] = jnp.where(
            low, acc_ref[pl.ds(2 * g * tq, tq), :], acc_ref[pl.ds((2 * g + 1) * tq, tq), :]
        ).astype(o_ref.dtype)


def _sb_attention(proj3, tq, n_heads):
    b, s, _ = proj3.shape
    width = n_heads * SB_HEAD_DIM
    groups = SB_HEADS // n_heads
    resident = pl.Buffered(1)
    return pl.pallas_call(
        functools.partial(_sb_kernel, tq=tq, n_heads=n_heads),
        out_shape=jax.ShapeDtypeStruct((b, s, SB_HEADS * SB_HEAD_DIM), BF16),
        grid=(b, groups, s // tq),
        in_specs=[pl.BlockSpec((None, tq, width), lambda bi, g, i: (bi, i, g)),
                  pl.BlockSpec((None, s, width), lambda bi, g, i: (bi, 0, groups + g),
                               pipeline_mode=resident),
                  pl.BlockSpec((None, s, width), lambda bi, g, i: (bi, 0, 2 * groups + g),
                               pipeline_mode=resident)],
        out_specs=pl.BlockSpec((None, tq, width), lambda bi, g, i: (bi, i, g)),
        scratch_shapes=[pltpu.VMEM((n_heads * tq, LANES), F32), pltpu.VMEM((n_heads * tq, 1), F32)],
        compiler_params=_cparams(("parallel", "parallel", "arbitrary")),
        name="sb_attn",
    )(proj3, proj3, proj3)


def _ret_kernel(lg_ref, q_ref, k_ref, v_ref, g_ref, cos_ref, sin_ref, o_ref,
                state_ref, mask_ref, qdec_ref, kdec_ref, *, tb):
    h = pl.program_id(1)
    si = pl.program_id(2)
    lg = lg_ref[h]
    half = RET_QK_DIM // 2

    @pl.when(si == 0)
    def _():
        state_ref[...] = jnp.zeros_like(state_ref)
        i = lax.broadcasted_iota(jnp.int32, (tb, tb), 0)
        j = lax.broadcasted_iota(jnp.int32, (tb, tb), 1)
        dist = jnp.abs(i - j).astype(F32)
        visible = (j // RET_CHUNK) <= (i // RET_CHUNK)
        mask_ref[...] = jnp.where(visible, jnp.exp(lg * dist), 0.0)
        r = lax.broadcasted_iota(jnp.int32, (tb, 1), 0).astype(F32)
        qdec_ref[...] = jnp.exp(lg * (r + 1.0))
        kdec_ref[...] = jnp.exp(lg * (tb - 1.0 - r))

    cos = cos_ref[...]
    sin = sin_ref[...]

    def rot(t):
        t1 = t[:, :half]
        t2 = t[:, half:]
        return jnp.concatenate([t1 * cos - t2 * sin, t1 * sin + t2 * cos], axis=-1)

    qr = rot(q_ref[...].astype(F32))
    kr = rot(k_ref[...].astype(F32))
    v = v_ref[...]
    scores = lax.dot_general(qr.astype(BF16), kr.astype(BF16), (((1,), (1,)), ((), ())),
                             preferred_element_type=F32) * mask_ref[...]
    state = state_ref[...]
    y = (jnp.dot(scores.astype(BF16), v, preferred_element_type=F32)
         + jnp.dot((qr * qdec_ref[...]).astype(BF16), state.astype(BF16),
                   preferred_element_type=F32))
    kd = (kr * kdec_ref[...]).astype(BF16)
    state_ref[...] = state * jnp.exp(lg * tb) + lax.dot_general(
        kd, v, (((0,), (0,)), ((), ())), preferred_element_type=F32)

    mu = jnp.mean(y, axis=-1, keepdims=True)
    yc = y - mu
    var = jnp.mean(yc * yc, axis=-1, keepdims=True)
    yn = yc * lax.rsqrt(var + LN_EPS)
    g = g_ref[...].astype(F32)
    o_ref[...] = (g * jax.nn.sigmoid(g) * yn).astype(o_ref.dtype)


def _retention(proj3, cos, sin, tb, off_q, off_k, off_v, off_g):
    b, s, _ = proj3.shape
    log_gamma = jnp.asarray(np.log(1.0 - 2.0 ** (-5.0 - np.arange(RET_HEADS))), F32)
    qb, kb = off_q // RET_QK_DIM, off_k // RET_QK_DIM
    vb, gb = off_v // RET_V_DIM, off_g // RET_V_DIM
    half = RET_QK_DIM // 2
    grid_spec = pltpu.PrefetchScalarGridSpec(
        num_scalar_prefetch=1,
        grid=(b, RET_HEADS, s // tb),
        in_specs=[pl.BlockSpec((None, tb, RET_QK_DIM), lambda bi, h, i, lg: (bi, i, qb + h)),
                  pl.BlockSpec((None, tb, RET_QK_DIM), lambda bi, h, i, lg: (bi, i, kb + h)),
                  pl.BlockSpec((None, tb, RET_V_DIM), lambda bi, h, i, lg: (bi, i, vb + h)),
                  pl.BlockSpec((None, tb, RET_V_DIM), lambda bi, h, i, lg: (bi, i, gb + h)),
                  pl.BlockSpec((tb, half), lambda bi, h, i, lg: (i, 0)),
                  pl.BlockSpec((tb, half), lambda bi, h, i, lg: (i, 0))],
        out_specs=pl.BlockSpec((None, tb, RET_V_DIM), lambda bi, h, i, lg: (bi, i, h)),
        scratch_shapes=[pltpu.VMEM((RET_QK_DIM, RET_V_DIM), F32),
                        pltpu.VMEM((tb, tb), F32),
                        pltpu.VMEM((tb, 1), F32),
                        pltpu.VMEM((tb, 1), F32)])
    return pl.pallas_call(
        functools.partial(_ret_kernel, tb=tb),
        out_shape=jax.ShapeDtypeStruct((b, s, RET_HEADS * RET_V_DIM), BF16),
        grid_spec=grid_spec,
        compiler_params=_cparams(("parallel", "parallel", "arbitrary")),
        name="retention",
    )(log_gamma, proj3, proj3, proj3, proj3, cos, sin)


def _layer_norm(h, g, b):
    mu = jnp.mean(h, axis=-1, keepdims=True)
    hc = h - mu
    var = jnp.mean(hc * hc, axis=-1, keepdims=True)
    return hc * lax.rsqrt(var + LN_EPS) * g + b


def _merge_kernel(ysb_ref, yret_ref, gsb_ref, gret_ref, x_ref, wsb_ref, wret_ref, wout_ref,
                  g1_ref, b1_ref, wrh_ref, wrl_ref, br_ref,
                  x1_ref, route_ref, topw_ref, cnt_ref, run_ref, *, tm):
    i = pl.program_id(0)

    @pl.when(i == 0)
    def _():
        run_ref[...] = jnp.zeros_like(run_ref)

    bsb = jnp.dot(ysb_ref[...], wsb_ref[...], preferred_element_type=F32)
    bret = jnp.dot(yret_ref[...], wret_ref[...], preferred_element_type=F32)
    merged = (jax.nn.sigmoid(gsb_ref[...].astype(F32)) * bsb
              + jax.nn.sigmoid(gret_ref[...].astype(F32)) * bret)
    mix = jnp.dot(merged.astype(BF16), wout_ref[...], preferred_element_type=F32)
    x1 = _layer_norm(DEEPNORM_ALPHA * x_ref[...] + mix, g1_ref[...], b1_ref[...])
    x1_ref[...] = x1

    x_hi = x1.astype(BF16)
    x_lo = (x1 - x_hi.astype(F32)).astype(BF16)
    logits = (jnp.dot(x_hi, wrh_ref[...], preferred_element_type=F32)
              + jnp.dot(x_hi, wrl_ref[...], preferred_element_type=F32)
              + jnp.dot(x_lo, wrh_ref[...], preferred_element_type=F32)) + br_ref[...]
    lane = lax.broadcasted_iota(jnp.int32, (tm, LANES), 1)
    lane_f = lane.astype(F32)
    neg = jnp.asarray(-jnp.inf, F32)
    work = jnp.where(lane < N_EXPERTS, logits, neg)
    vals, idxs = [], []
    sel = jnp.zeros((tm, LANES), F32)
    for _ in range(TOP_K):
        m = jnp.max(work, axis=-1, keepdims=True)
        idx = jnp.min(jnp.where(work == m, lane_f, float(LANES)), axis=-1, keepdims=True)
        hit = lane_f == idx
        sel = jnp.where(hit, 1.0, sel)
        work = jnp.where(hit, neg, work)
        vals.append(m)
        idxs.append(idx)
    exps = [jnp.exp(v - vals[0]) for v in vals]
    denom = exps[0] + exps[1] + exps[2] + exps[3]
    weights = [e / denom for e in exps]

    r = lax.broadcasted_iota(jnp.int32, (tm, tm), 0)
    c = lax.broadcasted_iota(jnp.int32, (tm, tm), 1)
    before = (c < r).astype(BF16)
    rank_mat = jnp.dot(before, sel.astype(BF16), preferred_element_type=F32) + run_ref[...]
    route = jnp.zeros((tm, LANES), F32)
    topw = jnp.zeros((tm, LANES), F32)
    for k in range(TOP_K):
        rank_k = jnp.sum(jnp.where(lane_f == idxs[k], rank_mat, 0.0), axis=-1, keepdims=True)
        route = jnp.where(lane == k, idxs[k], route)
        route = jnp.where(lane == TOP_K + k, rank_k, route)
        topw = jnp.where(lane == k, weights[k], topw)
    route_ref[...] = route.T[:2 * TOP_K, :].astype(jnp.int32)
    topw_ref[...] = topw
    run = run_ref[...] + jnp.sum(sel, axis=0, keepdims=True)
    run_ref[...] = run
    cnt_ref[...] = jnp.broadcast_to(run, cnt_ref.shape).astype(jnp.int32)


def _merge_route(y_sb, y_ret, proj, x2, wsb, wret, wout, g1, b1, wr_hi, wr_lo, br, tm,
                 off_gsb, off_gret):
    n, d = x2.shape
    const = lambda i: (0, 0)
    row = lambda i: (i, 0)
    return pl.pallas_call(
        functools.partial(_merge_kernel, tm=tm),
        out_shape=(jax.ShapeDtypeStruct((n, d), F32),
                   jax.ShapeDtypeStruct((2 * TOP_K, n), jnp.int32),
                   jax.ShapeDtypeStruct((n, LANES), F32),
                   jax.ShapeDtypeStruct((8, LANES), jnp.int32)),
        grid=(n // tm,),
        in_specs=[pl.BlockSpec((tm, y_sb.shape[1]), row),
                  pl.BlockSpec((tm, y_ret.shape[1]), row),
                  pl.BlockSpec((tm, d), lambda i: (i, off_gsb // d)),
                  pl.BlockSpec((tm, d), lambda i: (i, off_gret // d)),
                  pl.BlockSpec((tm, d), row),
                  pl.BlockSpec(wsb.shape, const),
                  pl.BlockSpec(wret.shape, const),
                  pl.BlockSpec(wout.shape, const),
                  pl.BlockSpec((1, d), const),
                  pl.BlockSpec((1, d), const),
                  pl.BlockSpec(wr_hi.shape, const),
                  pl.BlockSpec(wr_lo.shape, const),
                  pl.BlockSpec((1, LANES), const)],
        out_specs=(pl.BlockSpec((tm, d), row),
                   pl.BlockSpec((2 * TOP_K, tm), lambda i: (0, i)),
                   pl.BlockSpec((tm, LANES), row),
                   pl.BlockSpec((8, LANES), const)),
        scratch_shapes=[pltpu.VMEM((1, LANES), F32)],
        compiler_params=_cparams(("arbitrary",)),
        name="merge_route",
    )(y_sb, y_ret, proj, proj, x2, wsb, wret, wout, g1, b1, wr_hi, wr_lo, br)


PAD_CHUNKS = tuple(1 << b for b in reversed(range(SUBLANES.bit_length() - 1,
                                                  EXPERT_BLOCK.bit_length() - 1)))


ISSUE_UNROLL = 4


def _dispatch_kernel(padstart_ref, padlen_ref, slot_ref, x_ref, xs_hbm,
                     zero_ref, sem, zsem, *, tm):
    def pad_copies(act):
        def zero_copy(off, size):
            return pltpu.make_async_copy(zero_ref.at[pl.ds(0, size)],
                                         xs_hbm.at[pl.ds(off, size)], zsem)

        def per_expert(e, c):
            n = padlen_ref[e]
            off = padstart_ref[e]
            for r in range(SUBLANES - 1):
                @pl.when(r < (n & (SUBLANES - 1)))
                def _():
                    act(zero_copy(off + r, 1))
            off = pl.multiple_of(off + (n & (SUBLANES - 1)), SUBLANES)
            for size in PAD_CHUNKS:
                @pl.when((n & size) != 0)
                def _():
                    act(zero_copy(off, size))
                off = pl.multiple_of(off + (n & size), SUBLANES)
            return c

        lax.fori_loop(0, N_EXPERTS, per_expert, 0)

        def tail_chunk(j, c):
            off = pl.multiple_of(padstart_ref[N_EXPERTS] + j * PAD_CHUNKS[0], PAD_CHUNKS[0])
            act(zero_copy(off, PAD_CHUNKS[0]))
            return c

        lax.fori_loop(0, padlen_ref[N_EXPERTS] // PAD_CHUNKS[0], tail_chunk, 0)

    @pl.when(pl.program_id(0) == 0)
    def _():
        zero_ref[...] = jnp.zeros_like(zero_ref)
        pad_copies(lambda cp: cp.start())
        pad_copies(lambda cp: cp.wait())

    def issue(t, c):
        for k in range(TOP_K):
            pltpu.make_async_copy(x_ref.at[pl.ds(t, 1)],
                                  xs_hbm.at[pl.ds(slot_ref[k, t], 1)], sem).start()
        return c

    lax.fori_loop(0, tm, issue, 0, unroll=ISSUE_UNROLL)
    for _ in range(TOP_K):
        pltpu.make_async_copy(x_ref, xs_hbm.at[pl.ds(0, tm)], sem).wait()


def _dispatch(pad_start, pad_len, slots, x1, n_rows, tm):
    n, d = x1.shape
    grid_spec = pltpu.PrefetchScalarGridSpec(
        num_scalar_prefetch=2,
        grid=(n // tm,),
        in_specs=[pl.BlockSpec((TOP_K, tm), lambda i, *_: (0, i), memory_space=pltpu.SMEM),
                  pl.BlockSpec((tm, d), lambda i, *_: (i, 0))],
        out_specs=pl.BlockSpec(memory_space=pl.ANY),
        scratch_shapes=[pltpu.VMEM((PAD_CHUNKS[0], d), x1.dtype),
                        pltpu.SemaphoreType.DMA, pltpu.SemaphoreType.DMA])
    return pl.pallas_call(
        functools.partial(_dispatch_kernel, tm=tm),
        out_shape=jax.ShapeDtypeStruct((n_rows, d), x1.dtype),
        grid_spec=grid_spec,
        compiler_params=_cparams(("arbitrary",)),
        name="dispatch",
    )(pad_start, pad_len, slots, x1)


def _expert_kernel(be_ref, nb_ref, xs_ref, wgu_ref, bgu_ref, wd_ref, bd_ref, o_ref):
    del be_ref
    i = pl.program_id(0)

    @pl.when(i < nb_ref[0])
    def _():
        hgu = jnp.dot(xs_ref[...].astype(BF16), wgu_ref[...].astype(BF16),
                      preferred_element_type=F32) + bgu_ref[...]
        d_ff = hgu.shape[1] // 2
        gate = jnp.minimum(hgu[:, :d_ff], SWIGLU_LIMIT)
        up = jnp.clip(hgu[:, d_ff:], -SWIGLU_LIMIT, SWIGLU_LIMIT)
        glu = gate * jax.nn.sigmoid(gate * SWIGLU_ALPHA)
        act = ((up + 1.0) * glu).astype(BF16)
        o_ref[...] = jnp.dot(act, wd_ref[...].astype(BF16),
                             preferred_element_type=F32) + bd_ref[...]

    @pl.when(i >= nb_ref[0])
    def _():
        o_ref[...] = jnp.zeros_like(o_ref)


def _experts(block_expert, n_used, xs, wgu, bgu, wd, bd):
    n_rows, w = xs.shape
    d = wd.shape[2]
    n_blocks = n_rows // EXPERT_BLOCK
    grid_spec = pltpu.PrefetchScalarGridSpec(
        num_scalar_prefetch=2,
        grid=(n_blocks,),
        in_specs=[pl.BlockSpec((EXPERT_BLOCK, w), lambda i, be, nb: (jnp.minimum(i, nb[0] - 1), 0)),
                  pl.BlockSpec((None,) + wgu.shape[1:], lambda i, be, nb: (be[i], 0, 0)),
                  pl.BlockSpec((None,) + bgu.shape[1:], lambda i, be, nb: (be[i], 0, 0)),
                  pl.BlockSpec((None,) + wd.shape[1:], lambda i, be, nb: (be[i], 0, 0)),
                  pl.BlockSpec((None,) + bd.shape[1:], lambda i, be, nb: (be[i], 0, 0))],
        out_specs=pl.BlockSpec((EXPERT_BLOCK, d), lambda i, be, nb: (i, 0)))
    return pl.pallas_call(
        _expert_kernel,
        out_shape=jax.ShapeDtypeStruct((n_rows, d), F32),
        grid_spec=grid_spec,
        compiler_params=_cparams(("arbitrary",)),
        name="experts",
    )(block_expert, n_used, xs, wgu, bgu, wd, bd)


def _combine_kernel(slot_ref, slot_next_ref,
                    x1_ref, topw_ref, g2_ref, b2_ref, rows_hbm,
                    o_ref, buf_ref, sem, *, tc):
    i = pl.program_id(0)

    def issue_tile(s_ref, sl):
        def issue(t, c):
            for k in range(TOP_K):
                pltpu.make_async_copy(rows_hbm.at[pl.ds(s_ref[k, t], 1)],
                                      buf_ref.at[sl, k, pl.ds(t, 1)], sem.at[sl]).start()
            return c

        lax.fori_loop(0, tc, issue, 0, unroll=ISSUE_UNROLL)

    @pl.when(i == 0)
    def _():
        issue_tile(slot_ref, 0)

    def step(slot):
        @pl.when(i + 1 < pl.num_programs(0))
        def _():
            issue_tile(slot_next_ref, 1 - slot)

        for k in range(TOP_K):
            pltpu.make_async_copy(rows_hbm.at[pl.ds(0, tc)], buf_ref.at[slot, k],
                                  sem.at[slot]).wait()

        topw = topw_ref[...]
        y = topw[:, 0:1] * buf_ref[slot, 0]
        for k in range(1, TOP_K):
            y = y + topw[:, k:k + 1] * buf_ref[slot, k]
        o_ref[...] = _layer_norm(DEEPNORM_ALPHA * x1_ref[...] + y, g2_ref[...], b2_ref[...])

    for parity in range(2):
        @pl.when(i % 2 == parity)
        def _():
            step(parity)


def _combine(slots, x1, topw, g2, b2, rows, tc):
    n, d = x1.shape
    nt = n // tc
    return pl.pallas_call(
        functools.partial(_combine_kernel, tc=tc),
        out_shape=jax.ShapeDtypeStruct((n, d), F32),
        grid=(nt,),
        in_specs=[pl.BlockSpec((TOP_K, tc), lambda i: (0, i), memory_space=pltpu.SMEM),
                  pl.BlockSpec((TOP_K, tc), lambda i: (0, jnp.minimum(i + 1, nt - 1)),
                               memory_space=pltpu.SMEM),
                  pl.BlockSpec((tc, d), lambda i: (i, 0)),
                  pl.BlockSpec((tc, LANES), lambda i: (i, 0)),
                  pl.BlockSpec((1, d), lambda i: (0, 0)),
                  pl.BlockSpec((1, d), lambda i: (0, 0)),
                  pl.BlockSpec(memory_space=pl.ANY)],
        out_specs=pl.BlockSpec((tc, d), lambda i: (i, 0)),
        scratch_shapes=[pltpu.VMEM((2, TOP_K, tc, d), F32), pltpu.SemaphoreType.DMA((2,))],
        compiler_params=_cparams(("arbitrary",)),
        name="combine",
    )(slots, slots, x1, topw, g2, b2, rows)


def _tile(n, pref):
    t = min(n, pref)
    assert n % t == 0, (n, t)
    return t


def kernel(x, w_in, w_branch_sb, w_branch_ret, w_out, ln1_g, ln1_b, w_router, b_router,
           w_gate_up, b_gate_up, w_down, b_down, ln2_g, ln2_b):
    b, s, d = x.shape
    n = b * s
    sb_w = SB_HEADS * SB_HEAD_DIM
    qk_w = RET_HEADS * RET_QK_DIM
    v_w = RET_HEADS * RET_V_DIM
    sizes = (sb_w, sb_w, sb_w, qk_w, qk_w, v_w, v_w, d, d)
    offs = [int(o) for o in np.cumsum((0,) + sizes)]
    assert w_in.shape[0] == DEPTH and w_in.shape[2] == offs[-1]
    assert d == sb_w and d % (2 * LANES) == 0

    half = RET_QK_DIM // 2
    inv_freq = ROPE_BASE ** (-jnp.arange(half, dtype=F32) / half)
    ang = jnp.arange(s, dtype=F32)[:, None] * inv_freq[None, :]
    cos, sin = jnp.cos(ang), jnp.sin(ang)

    n_assign = n * TOP_K
    n_rows = -(-n_assign // EXPERT_BLOCK) * EXPERT_BLOCK + N_EXPERTS * EXPERT_BLOCK
    n_blocks = n_rows // EXPERT_BLOCK

    col_scale = (jnp.ones((1, offs[-1]), F32).at[:, :sb_w].set(SB_HEAD_DIM ** -0.5 * LOG2_E)
                 .at[:, offs[3]:offs[4]].set(RET_QK_DIM ** -0.5))

    xcur = x.reshape(n, d)
    for l in range(DEPTH):
        proj = _in_proj(xcur, w_in[l].astype(BF16), col_scale, _tile(n, 1024), offs[-1] // 4)
        proj3 = proj.reshape(b, s, offs[-1])
        y_sb = _sb_attention(proj3, _tile(s, 128), SB_HEADS)
        y_ret = _retention(proj3, cos, sin, _tile(s, 512), offs[3], offs[4], offs[5], offs[6])

        wr = jnp.zeros((d, LANES), F32).at[:, :N_EXPERTS].set(w_router[l])
        br = jnp.zeros((1, LANES), F32).at[0, :N_EXPERTS].set(b_router[l])
        wr_hi = wr.astype(BF16)
        wr_lo = (wr - wr_hi.astype(F32)).astype(BF16)
        x1, route, topw, cnt = _merge_route(
            y_sb.reshape(n, sb_w), y_ret.reshape(n, v_w), proj, xcur,
            w_branch_sb[l].astype(BF16), w_branch_ret[l].astype(BF16), w_out[l].astype(BF16),
            ln1_g[l].reshape(1, d), ln1_b[l].reshape(1, d), wr_hi, wr_lo, br,
            _tile(n, 512), offs[7], offs[8])

        counts = cnt[0, :N_EXPERTS]
        padded = (counts + EXPERT_BLOCK - 1) // EXPERT_BLOCK * EXPERT_BLOCK
        padded_ends = jnp.cumsum(padded)
        starts = (padded_ends - padded).astype(jnp.int32)
        n_used = (padded_ends[-1:] // EXPERT_BLOCK).astype(jnp.int32)
        block_rows = jnp.arange(n_blocks, dtype=jnp.int32) * EXPERT_BLOCK
        block_expert = jnp.minimum(
            jnp.sum((padded_ends[None, :] <= block_rows[:, None]).astype(jnp.int32), axis=1),
            N_EXPERTS - 1).astype(jnp.int32)
        pad_start = jnp.concatenate([starts + counts, padded_ends[-1:]]).astype(jnp.int32)
        pad_len = jnp.concatenate([padded - counts, n_rows - padded_ends[-1:]]).astype(jnp.int32)

        expert_ids = jnp.arange(N_EXPERTS, dtype=jnp.int32)[:, None, None]
        slots = route[TOP_K:] + jnp.sum(
            jnp.where(route[None, :TOP_K] == expert_ids, starts[:, None, None], 0), axis=0)

        xs = _dispatch(pad_start, pad_len, slots, x1, n_rows, _tile(n, 1024))
        rows = _experts(block_expert, n_used, xs,
                        w_gate_up[l], b_gate_up[l][:, None, :],
                        w_down[l], b_down[l][:, None, :])
        xcur = _combine(slots, x1, topw, ln2_g[l].reshape(1, d),
                        ln2_b[l].reshape(1, d), rows, _tile(n, 512))
    return xcur.reshape(b, s, d)
```

```python
import functools

import numpy as np
import jax
import jax.numpy as jnp
from jax import lax
from jax.experimental import pallas as pl
from jax.experimental.pallas import tpu as pltpu

F32 = jnp.float32
BF16 = jnp.bfloat16

SB_HEADS = 16
SB_HEAD_DIM = 64
RET_HEADS = 4
RET_QK_DIM = 256
RET_V_DIM = 512
RET_CHUNK = 64
ROPE_BASE = 10000.0
N_EXPERTS = 32
TOP_K = 4
EXPERT_BLOCK = 512
SWIGLU_LIMIT = 7.0
SWIGLU_ALPHA = 1.702
LN_EPS = 1e-5
DEPTH = 1
DEEPNORM_ALPHA = (2.0 * DEPTH) ** 0.25

LANES = 128
SUBLANES = 8
VMEM_LIMIT = 56 * 1024 * 1024

EXP2_UNDERFLOW_BITS = 151.0
LOG2_E = 1.4426950408889634
MASKED_SCORE = -1e30


def _cparams(sem):
    return pltpu.CompilerParams(dimension_semantics=sem, vmem_limit_bytes=VMEM_LIMIT)


def _proj_kernel(x_ref, w_ref, cs_ref, o_ref, xb_ref):
    @pl.when(pl.program_id(1) == 0)
    def _():
        xb_ref[...] = x_ref[...].astype(BF16)

    acc = jnp.dot(xb_ref[...], w_ref[...], preferred_element_type=F32)
    o_ref[...] = (acc * cs_ref[...]).astype(o_ref.dtype)


def _in_proj(x2, w_bf, col_scale, tm, tn):
    n, d = x2.shape
    width = w_bf.shape[1]
    return pl.pallas_call(
        _proj_kernel,
        out_shape=jax.ShapeDtypeStruct((n, width), BF16),
        grid=(n // tm, width // tn),
        in_specs=[pl.BlockSpec((tm, d), lambda i, j: (i, 0)),
                  pl.BlockSpec((d, tn), lambda i, j: (0, j)),
                  pl.BlockSpec((1, tn), lambda i, j: (0, j))],
        out_specs=pl.BlockSpec((tm, tn), lambda i, j: (i, j)),
        scratch_shapes=[pltpu.VMEM((tm, d), BF16)],
        compiler_params=_cparams(("parallel", "arbitrary")),
        name="in_proj",
    )(x2, w_bf, col_scale)


def _sb_kernel(q_ref, k_ref, v_ref, o_ref, acc_ref, carry_ref, *, tq, n_heads):
    qi = pl.program_id(2)
    n_groups = n_heads // 2
    lane = lax.broadcasted_iota(jnp.int32, (tq, LANES), 1)
    low = lane < SB_HEAD_DIM
    q = q_ref[...]
    zero = jnp.zeros((tq, LANES), BF16)
    q_pairs = []
    for g in range(n_groups):
        qg = q[:, g * LANES:(g + 1) * LANES]
        q_pairs.append(jnp.concatenate([jnp.where(low, qg, zero), jnp.where(low, zero, qg)], axis=0))
    def suffix_matrix(width):
        row = lax.broadcasted_iota(jnp.int32, (width, width), 0)
        col = lax.broadcasted_iota(jnp.int32, (width, width), 1)
        return (row >= col).astype(BF16)

    wide = 2 * tq
    suffix = {tq: suffix_matrix(tq), wide: suffix_matrix(wide)}
    rows_all = lax.broadcasted_iota(jnp.int32, (n_heads * tq, tq), 0)
    cols_all = lax.broadcasted_iota(jnp.int32, (n_heads * tq, tq), 1)
    strict = cols_all < (rows_all & (tq - 1))

    acc_ref[...] = jnp.zeros_like(acc_ref)
    carry_ref[...] = jnp.zeros_like(carry_ref)

    def block(start, width, diag):
        keys = pl.ds(pl.multiple_of(start, tq), width)
        z = jnp.concatenate(
            [lax.dot_general(q_pairs[g], k_ref[keys, pl.ds(g * LANES, LANES)],
                             (((1,), (1,)), ((), ())), preferred_element_type=F32)
             for g in range(n_groups)], axis=0)
        if diag:
            z = jnp.where(strict, z, MASKED_SCORE)
        sp = jnp.maximum(z, 0.0) + jnp.log2(1.0 + jnp.exp2(-jnp.abs(z)))
        cum = jnp.dot(sp.astype(BF16), suffix[width], preferred_element_type=F32)
        carry = carry_ref[...]
        a = jnp.exp2(z - cum - carry).astype(BF16)
        pv = jnp.concatenate(
            [jnp.dot(a[g * 2 * tq:(g + 1) * 2 * tq], v_ref[keys, pl.ds(g * LANES, LANES)],
                     preferred_element_type=F32)
             for g in range(n_groups)], axis=0)
        acc_ref[...] += pv
        carry_ref[...] = carry + cum[:, 0:1]

    def alive():
        return jnp.min(carry_ref[...]) <= EXP2_UNDERFLOW_BITS

    block(qi * tq, tq, True)

    n_wide = qi // 2

    def cond(c):
        j, live = c
        return jnp.logical_and(j < n_wide, live)

    def body(c):
        j, _ = c
        block(qi * tq - wide * (j + 1), wide, False)
        return j + 1, alive()

    _, live = lax.while_loop(cond, body, (0, alive()))

    @pl.when(jnp.logical_and(live, qi % 2 == 1))
    def _():
        block(0, tq, False)
    for g in range(n_groups):
        o_ref[:, g * LANES:(g + 1) * LANES] = jnp.where(
            low, acc_ref[pl.ds(2 * g * tq, tq), :], acc_ref[pl.ds((2 * g + 1) * tq, tq), :]
        ).astype(o_ref.dtype)


def _sb_attention(proj3, tq, n_heads):
    b, s, _ = proj3.shape
    width = n_heads * SB_HEAD_DIM
    groups = SB_HEADS // n_heads
    resident = pl.Buffered(1)
    return pl.pallas_call(
        functools.partial(_sb_kernel, tq=tq, n_heads=n_heads),
        out_shape=jax.ShapeDtypeStruct((b, s, SB_HEADS * SB_HEAD_DIM), BF16),
        grid=(b, groups, s // tq),
        in_specs=[pl.BlockSpec((None, tq, width), lambda bi, g, i: (bi, i, g)),
                  pl.BlockSpec((None, s, width), lambda bi, g, i: (bi, 0, groups + g),
                               pipeline_mode=resident),
                  pl.BlockSpec((None, s, width), lambda bi, g, i: (bi, 0, 2 * groups + g),
                               pipeline_mode=resident)],
        out_specs=pl.BlockSpec((None, tq, width), lambda bi, g, i: (bi, i, g)),
        scratch_shapes=[pltpu.VMEM((n_heads * tq, LANES), F32), pltpu.VMEM((n_heads * tq, 1), F32)],
        compiler_params=_cparams(("parallel", "parallel", "arbitrary")),
        name="sb_attn",
    )(proj3, proj3, proj3)


def _ret_kernel(lg_ref, q_ref, k_ref, v_ref, g_ref, cos_ref, sin_ref, o_ref,
                state_ref, mask_ref, qdec_ref, kdec_ref, *, tb):
    h = pl.program_id(1)
    si = pl.program_id(2)
    lg = lg_ref[h]
    half = RET_QK_DIM // 2

    @pl.when(si == 0)
    def _():
        state_ref[...] = jnp.zeros_like(state_ref)
        i = lax.broadcasted_iota(jnp.int32, (tb, tb), 0)
        j = lax.broadcasted_iota(jnp.int32, (tb, tb), 1)
        dist = jnp.abs(i - j).astype(F32)
        visible = (j // RET_CHUNK) <= (i // RET_CHUNK)
        mask_ref[...] = jnp.where(visible, jnp.exp(lg * dist), 0.0)
        r = lax.broadcasted_iota(jnp.int32, (tb, 1), 0).astype(F32)
        qdec_ref[...] = jnp.exp(lg * (r + 1.0))
        kdec_ref[...] = jnp.exp(lg * (tb - 1.0 - r))

    cos = cos_ref[...]
    sin = sin_ref[...]

    def rot(t):
        t1 = t[:, :half]
        t2 = t[:, half:]
        return jnp.concatenate([t1 * cos - t2 * sin, t1 * sin + t2 * cos], axis=-1)

    qr = rot(q_ref[...].astype(F32))
    kr = rot(k_ref[...].astype(F32))
    v = v_ref[...]
    scores = lax.dot_general(qr.astype(BF16), kr.astype(BF16), (((1,), (1,)), ((), ())),
                             preferred_element_type=F32) * mask_ref[...]
    state = state_ref[...]
    y = (jnp.dot(scores.astype(BF16), v, preferred_element_type=F32)
         + jnp.dot((qr * qdec_ref[...]).astype(BF16), state.astype(BF16),
                   preferred_element_type=F32))
    kd = (kr * kdec_ref[...]).astype(BF16)
    state_ref[...] = state * jnp.exp(lg * tb) + lax.dot_general(
        kd, v, (((0,), (0,)), ((), ())), preferred_element_type=F32)

    mu = jnp.mean(y, axis=-1, keepdims=True)
    yc = y - mu
    var = jnp.mean(yc * yc, axis=-1, keepdims=True)
    yn = yc * lax.rsqrt(var + LN_EPS)
    g = g_ref[...].astype(F32)
    o_ref[...] = (g * jax.nn.sigmoid(g) * yn).astype(o_ref.dtype)


def _retention(proj3, cos, sin, tb, off_q, off_k, off_v, off_g):
    b, s, _ = proj3.shape
    log_gamma = jnp.asarray(np.log(1.0 - 2.0 ** (-5.0 - np.arange(RET_HEADS))), F32)
    qb, kb = off_q // RET_QK_DIM, off_k // RET_QK_DIM
    vb, gb = off_v // RET_V_DIM, off_g // RET_V_DIM
    half = RET_QK_DIM // 2
    grid_spec = pltpu.PrefetchScalarGridSpec(
        num_scalar_prefetch=1,
        grid=(b, RET_HEADS, s // tb),
        in_specs=[pl.BlockSpec((None, tb, RET_QK_DIM), lambda bi, h, i, lg: (bi, i, qb + h)),
                  pl.BlockSpec((None, tb, RET_QK_DIM), lambda bi, h, i, lg: (bi, i, kb + h)),
                  pl.BlockSpec((None, tb, RET_V_DIM), lambda bi, h, i, lg: (bi, i, vb + h)),
                  pl.BlockSpec((None, tb, RET_V_DIM), lambda bi, h, i, lg: (bi, i, gb + h)),
                  pl.BlockSpec((tb, half), lambda bi, h, i, lg: (i, 0)),
                  pl.BlockSpec((tb, half), lambda bi, h, i, lg: (i, 0))],
        out_specs=pl.BlockSpec((None, tb, RET_V_DIM), lambda bi, h, i, lg: (bi, i, h)),
        scratch_shapes=[pltpu.VMEM((RET_QK_DIM, RET_V_DIM), F32),
                        pltpu.VMEM((tb, tb), F32),
                        pltpu.VMEM((tb, 1), F32),
                        pltpu.VMEM((tb, 1), F32)])
    return pl.pallas_call(
        functools.partial(_ret_kernel, tb=tb),
        out_shape=jax.ShapeDtypeStruct((b, s, RET_HEADS * RET_V_DIM), BF16),
        grid_spec=grid_spec,
        compiler_params=_cparams(("parallel", "parallel", "arbitrary")),
        name="retention",
    )(log_gamma, proj3, proj3, proj3, proj3, cos, sin)


def _layer_norm(h, g, b):
    mu = jnp.mean(h, axis=-1, keepdims=True)
    hc = h - mu
    var = jnp.mean(hc * hc, axis=-1, keepdims=True)
    return hc * lax.rsqrt(var + LN_EPS) * g + b


def _merge_kernel(ysb_ref, yret_ref, gsb_ref, gret_ref, x_ref, wsb_ref, wret_ref, wout_ref,
                  g1_ref, b1_ref, wrh_ref, wrl_ref, br_ref,
                  x1_ref, route_ref, topw_ref, cnt_ref, run_ref, *, tm):
    i = pl.program_id(0)

    @pl.when(i == 0)
    def _():
        run_ref[...] = jnp.zeros_like(run_ref)

    bsb = jnp.dot(ysb_ref[...], wsb_ref[...], preferred_element_type=F32)
    bret = jnp.dot(yret_ref[...], wret_ref[...], preferred_element_type=F32)
    merged = (jax.nn.sigmoid(gsb_ref[...].astype(F32)) * bsb
              + jax.nn.sigmoid(gret_ref[...].astype(F32)) * bret)
    mix = jnp.dot(merged.astype(BF16), wout_ref[...], preferred_element_type=F32)
    x1 = _layer_norm(DEEPNORM_ALPHA * x_ref[...] + mix, g1_ref[...], b1_ref[...])
    x1_ref[...] = x1

    x_hi = x1.astype(BF16)
    x_lo = (x1 - x_hi.astype(F32)).astype(BF16)
    logits = (jnp.dot(x_hi, wrh_ref[...], preferred_element_type=F32)
              + jnp.dot(x_hi, wrl_ref[...], preferred_element_type=F32)
              + jnp.dot(x_lo, wrh_ref[...], preferred_element_type=F32)) + br_ref[...]
    lane = lax.broadcasted_iota(jnp.int32, (tm, LANES), 1)
    lane_f = lane.astype(F32)
    neg = jnp.asarray(-jnp.inf, F32)
    work = jnp.where(lane < N_EXPERTS, logits, neg)
    vals, idxs = [], []
    sel = jnp.zeros((tm, LANES), F32)
    for _ in range(TOP_K):
        m = jnp.max(work, axis=-1, keepdims=True)
        idx = jnp.min(jnp.where(work == m, lane_f, float(LANES)), axis=-1, keepdims=True)
        hit = lane_f == idx
        sel = jnp.where(hit, 1.0, sel)
        work = jnp.where(hit, neg, work)
        vals.append(m)
        idxs.append(idx)
    exps = [jnp.exp(v - vals[0]) for v in vals]
    denom = exps[0] + exps[1] + exps[2] + exps[3]
    weights = [e / denom for e in exps]

    r = lax.broadcasted_iota(jnp.int32, (tm, tm), 0)
    c = lax.broadcasted_iota(jnp.int32, (tm, tm), 1)
    before = (c < r).astype(BF16)
    rank_mat = jnp.dot(before, sel.astype(BF16), preferred_element_type=F32) + run_ref[...]
    route = jnp.zeros((tm, LANES), F32)
    topw = jnp.zeros((tm, LANES), F32)
    for k in range(TOP_K):
        rank_k = jnp.sum(jnp.where(lane_f == idxs[k], rank_mat, 0.0), axis=-1, keepdims=True)
        route = jnp.where(lane == k, idxs[k], route)
        route = jnp.where(lane == TOP_K + k, rank_k, route)
        topw = jnp.where(lane == k, weights[k], topw)
    route_ref[...] = route.T[:2 * TOP_K, :].astype(jnp.int32)
    topw_ref[...] = topw
    run = run_ref[...] + jnp.sum(sel, axis=0, keepdims=True)
    run_ref[...] = run
    cnt_ref[...] = jnp.broadcast_to(run, cnt_ref.shape).astype(jnp.int32)


def _merge_route(y_sb, y_ret, proj, x2, wsb, wret, wout, g1, b1, wr_hi, wr_lo, br, tm,
                 off_gsb, off_gret):
    n, d = x2.shape
    const = lambda i: (0, 0)
    row = lambda i: (i, 0)
    return pl.pallas_call(
        functools.partial(_merge_kernel, tm=tm),
        out_shape=(jax.ShapeDtypeStruct((n, d), F32),
                   jax.ShapeDtypeStruct((2 * TOP_K, n), jnp.int32),
                   jax.ShapeDtypeStruct((n, LANES), F32),
                   jax.ShapeDtypeStruct((8, LANES), jnp.int32)),
        grid=(n // tm,),
        in_specs=[pl.BlockSpec((tm, y_sb.shape[1]), row),
                  pl.BlockSpec((tm, y_ret.shape[1]), row),
                  pl.BlockSpec((tm, d), lambda i: (i, off_gsb // d)),
                  pl.BlockSpec((tm, d), lambda i: (i, off_gret // d)),
                  pl.BlockSpec((tm, d), row),
                  pl.BlockSpec(wsb.shape, const),
                  pl.BlockSpec(wret.shape, const),
                  pl.BlockSpec(wout.shape, const),
                  pl.BlockSpec((1, d), const),
                  pl.BlockSpec((1, d), const),
                  pl.BlockSpec(wr_hi.shape, const),
                  pl.BlockSpec(wr_lo.shape, const),
                  pl.BlockSpec((1, LANES), const)],
        out_specs=(pl.BlockSpec((tm, d), row),
                   pl.BlockSpec((2 * TOP_K, tm), lambda i: (0, i)),
                   pl.BlockSpec((tm, LANES), row),
                   pl.BlockSpec((8, LANES), const)),
        scratch_shapes=[pltpu.VMEM((1, LANES), F32)],
        compiler_params=_cparams(("arbitrary",)),
        name="merge_route",
    )(y_sb, y_ret, proj, proj, x2, wsb, wret, wout, g1, b1, wr_hi, wr_lo, br)


PAD_CHUNKS = tuple(1 << b for b in reversed(range(SUBLANES.bit_length() - 1,
                                                  EXPERT_BLOCK.bit_length() - 1)))


ISSUE_UNROLL = 4
DMA_PRIORITIES = 2


def _dispatch_kernel(padstart_ref, padlen_ref, slot_ref, x_ref, xs_hbm,
                     zero_ref, sem, zsem, *, tm):
    def pad_copies(act):
        def zero_copy(off, size):
            return pltpu.make_async_copy(zero_ref.at[pl.ds(0, size)],
                                         xs_hbm.at[pl.ds(off, size)], zsem)

        def per_expert(e, c):
            n = padlen_ref[e]
            off = padstart_ref[e]
            for r in range(SUBLANES - 1):
                @pl.when(r < (n & (SUBLANES - 1)))
                def _():
                    act(zero_copy(off + r, 1))
            off = pl.multiple_of(off + (n & (SUBLANES - 1)), SUBLANES)
            for size in PAD_CHUNKS:
                @pl.when((n & size) != 0)
                def _():
                    act(zero_copy(off, size))
                off = pl.multiple_of(off + (n & size), SUBLANES)
            return c

        lax.fori_loop(0, N_EXPERTS, per_expert, 0)

        def tail_chunk(j, c):
            off = pl.multiple_of(padstart_ref[N_EXPERTS] + j * PAD_CHUNKS[0], PAD_CHUNKS[0])
            act(zero_copy(off, PAD_CHUNKS[0]))
            return c

        lax.fori_loop(0, padlen_ref[N_EXPERTS] // PAD_CHUNKS[0], tail_chunk, 0)

    @pl.when(pl.program_id(0) == 0)
    def _():
        zero_ref[...] = jnp.zeros_like(zero_ref)
        pad_copies(lambda cp: cp.start())
        pad_copies(lambda cp: cp.wait())

    def issue(t, c):
        for k in range(TOP_K):
            pltpu.make_async_copy(x_ref.at[pl.ds(t, 1)],
                                  xs_hbm.at[pl.ds(slot_ref[k, t], 1)], sem
                                  ).start(priority=k % DMA_PRIORITIES)
        return c

    lax.fori_loop(0, tm, issue, 0, unroll=ISSUE_UNROLL)
    for _ in range(TOP_K):
        pltpu.make_async_copy(x_ref, xs_hbm.at[pl.ds(0, tm)], sem).wait()


def _dispatch(pad_start, pad_len, slots, x1, n_rows, tm):
    n, d = x1.shape
    grid_spec = pltpu.PrefetchScalarGridSpec(
        num_scalar_prefetch=2,
        grid=(n // tm,),
        in_specs=[pl.BlockSpec((TOP_K, tm), lambda i, *_: (0, i), memory_space=pltpu.SMEM),
                  pl.BlockSpec((tm, d), lambda i, *_: (i, 0))],
        out_specs=pl.BlockSpec(memory_space=pl.ANY),
        scratch_shapes=[pltpu.VMEM((PAD_CHUNKS[0], d), x1.dtype),
                        pltpu.SemaphoreType.DMA, pltpu.SemaphoreType.DMA])
    return pl.pallas_call(
        functools.partial(_dispatch_kernel, tm=tm),
        out_shape=jax.ShapeDtypeStruct((n_rows, d), x1.dtype),
        grid_spec=grid_spec,
        compiler_params=_cparams(("arbitrary",)),
        name="dispatch",
    )(pad_start, pad_len, slots, x1)


def _expert_kernel(be_ref, nb_ref, xs_ref, wgu_ref, bgu_ref, wd_ref, bd_ref, o_ref):
    del be_ref
    i = pl.program_id(0)

    @pl.when(i < nb_ref[0])
    def _():
        hgu = jnp.dot(xs_ref[...].astype(BF16), wgu_ref[...].astype(BF16),
                      preferred_element_type=F32) + bgu_ref[...]
        d_ff = hgu.shape[1] // 2
        gate = jnp.minimum(hgu[:, :d_ff], SWIGLU_LIMIT)
        up = jnp.clip(hgu[:, d_ff:], -SWIGLU_LIMIT, SWIGLU_LIMIT)
        glu = gate * jax.nn.sigmoid(gate * SWIGLU_ALPHA)
        act = ((up + 1.0) * glu).astype(BF16)
        o_ref[...] = jnp.dot(act, wd_ref[...].astype(BF16),
                             preferred_element_type=F32) + bd_ref[...]

    @pl.when(i >= nb_ref[0])
    def _():
        o_ref[...] = jnp.zeros_like(o_ref)


def _experts(block_expert, n_used, xs, wgu, bgu, wd, bd):
    n_rows, w = xs.shape
    d = wd.shape[2]
    n_blocks = n_rows // EXPERT_BLOCK
    grid_spec = pltpu.PrefetchScalarGridSpec(
        num_scalar_prefetch=2,
        grid=(n_blocks,),
        in_specs=[pl.BlockSpec((EXPERT_BLOCK, w), lambda i, be, nb: (jnp.minimum(i, nb[0] - 1), 0)),
                  pl.BlockSpec((None,) + wgu.shape[1:], lambda i, be, nb: (be[i], 0, 0)),
                  pl.BlockSpec((None,) + bgu.shape[1:], lambda i, be, nb: (be[i], 0, 0)),
                  pl.BlockSpec((None,) + wd.shape[1:], lambda i, be, nb: (be[i], 0, 0)),
                  pl.BlockSpec((None,) + bd.shape[1:], lambda i, be, nb: (be[i], 0, 0))],
        out_specs=pl.BlockSpec((EXPERT_BLOCK, d), lambda i, be, nb: (i, 0)))
    return pl.pallas_call(
        _expert_kernel,
        out_shape=jax.ShapeDtypeStruct((n_rows, d), F32),
        grid_spec=grid_spec,
        compiler_params=_cparams(("arbitrary",)),
        name="experts",
    )(block_expert, n_used, xs, wgu, bgu, wd, bd)


def _combine_kernel(slot_ref, slot_next_ref,
                    x1_ref, topw_ref, g2_ref, b2_ref, rows_hbm,
                    o_ref, buf_ref, sem, *, tc):
    i = pl.program_id(0)

    def issue_tile(s_ref, sl):
        def issue(t, c):
            for k in range(TOP_K):
                pltpu.make_async_copy(rows_hbm.at[pl.ds(s_ref[k, t], 1)],
                                      buf_ref.at[sl, k, pl.ds(t, 1)], sem.at[sl]
                                      ).start(priority=k % DMA_PRIORITIES)
            return c

        lax.fori_loop(0, tc, issue, 0, unroll=ISSUE_UNROLL)

    @pl.when(i == 0)
    def _():
        issue_tile(slot_ref, 0)

    def step(slot):
        @pl.when(i + 1 < pl.num_programs(0))
        def _():
            issue_tile(slot_next_ref, 1 - slot)

        for k in range(TOP_K):
            pltpu.make_async_copy(rows_hbm.at[pl.ds(0, tc)], buf_ref.at[slot, k],
                                  sem.at[slot]).wait()

        topw = topw_ref[...]
        y = topw[:, 0:1] * buf_ref[slot, 0]
        for k in range(1, TOP_K):
            y = y + topw[:, k:k + 1] * buf_ref[slot, k]
        o_ref[...] = _layer_norm(DEEPNORM_ALPHA * x1_ref[...] + y, g2_ref[...], b2_ref[...])

    for parity in range(2):
        @pl.when(i % 2 == parity)
        def _():
            step(parity)


def _combine(slots, x1, topw, g2, b2, rows, tc):
    n, d = x1.shape
    nt = n // tc
    return pl.pallas_call(
        functools.partial(_combine_kernel, tc=tc),
        out_shape=jax.ShapeDtypeStruct((n, d), F32),
        grid=(nt,),
        in_specs=[pl.BlockSpec((TOP_K, tc), lambda i: (0, i), memory_space=pltpu.SMEM),
                  pl.BlockSpec((TOP_K, tc), lambda i: (0, jnp.minimum(i + 1, nt - 1)),
                               memory_space=pltpu.SMEM),
                  pl.BlockSpec((tc, d), lambda i: (i, 0)),
                  pl.BlockSpec((tc, LANES), lambda i: (i, 0)),
                  pl.BlockSpec((1, d), lambda i: (0, 0)),
                  pl.BlockSpec((1, d), lambda i: (0, 0)),
                  pl.BlockSpec(memory_space=pl.ANY)],
        out_specs=pl.BlockSpec((tc, d), lambda i: (i, 0)),
        scratch_shapes=[pltpu.VMEM((2, TOP_K, tc, d), F32), pltpu.SemaphoreType.DMA((2,))],
        compiler_params=_cparams(("arbitrary",)),
        name="combine",
    )(slots, slots, x1, topw, g2, b2, rows)


def _tile(n, pref):
    t = min(n, pref)
    assert n % t == 0, (n, t)
    return t


def kernel(x, w_in, w_branch_sb, w_branch_ret, w_out, ln1_g, ln1_b, w_router, b_router,
           w_gate_up, b_gate_up, w_down, b_down, ln2_g, ln2_b):
    b, s, d = x.shape
    n = b * s
    sb_w = SB_HEADS * SB_HEAD_DIM
    qk_w = RET_HEADS * RET_QK_DIM
    v_w = RET_HEADS * RET_V_DIM
    sizes = (sb_w, sb_w, sb_w, qk_w, qk_w, v_w, v_w, d, d)
    offs = [int(o) for o in np.cumsum((0,) + sizes)]
    assert w_in.shape[0] == DEPTH and w_in.shape[2] == offs[-1]
    assert d == sb_w and d % (2 * LANES) == 0

    half = RET_QK_DIM // 2
    inv_freq = ROPE_BASE ** (-jnp.arange(half, dtype=F32) / half)
    ang = jnp.arange(s, dtype=F32)[:, None] * inv_freq[None, :]
    cos, sin = jnp.cos(ang), jnp.sin(ang)

    n_assign = n * TOP_K
    n_rows = -(-n_assign // EXPERT_BLOCK) * EXPERT_BLOCK + N_EXPERTS * EXPERT_BLOCK
    n_blocks = n_rows // EXPERT_BLOCK

    col_scale = (jnp.ones((1, offs[-1]), F32).at[:, :sb_w].set(SB_HEAD_DIM ** -0.5 * LOG2_E)
                 .at[:, offs[3]:offs[4]].set(RET_QK_DIM ** -0.5))

    xcur = x.reshape(n, d)
    for l in range(DEPTH):
        proj = _in_proj(xcur, w_in[l].astype(BF16), col_scale, _tile(n, 1024), offs[-1] // 4)
        proj3 = proj.reshape(b, s, offs[-1])
        y_sb = _sb_attention(proj3, _tile(s, 128), SB_HEADS)
        y_ret = _retention(proj3, cos, sin, _tile(s, 512), offs[3], offs[4], offs[5], offs[6])

        wr = jnp.zeros((d, LANES), F32).at[:, :N_EXPERTS].set(w_router[l])
        br = jnp.zeros((1, LANES), F32).at[0, :N_EXPERTS].set(b_router[l])
        wr_hi = wr.astype(BF16)
        wr_lo = (wr - wr_hi.astype(F32)).astype(BF16)
        x1, route, topw, cnt = _merge_route(
            y_sb.reshape(n, sb_w), y_ret.reshape(n, v_w), proj, xcur,
            w_branch_sb[l].astype(BF16), w_branch_ret[l].astype(BF16), w_out[l].astype(BF16),
            ln1_g[l].reshape(1, d), ln1_b[l].reshape(1, d), wr_hi, wr_lo, br,
            _tile(n, 512), offs[7], offs[8])

        counts = cnt[0, :N_EXPERTS]
        padded = (counts + EXPERT_BLOCK - 1) // EXPERT_BLOCK * EXPERT_BLOCK
        padded_ends = jnp.cumsum(padded)
        starts = (padded_ends - padded).astype(jnp.int32)
        n_used = (padded_ends[-1:] // EXPERT_BLOCK).astype(jnp.int32)
        block_rows = jnp.arange(n_blocks, dtype=jnp.int32) * EXPERT_BLOCK
        block_expert = jnp.minimum(
            jnp.sum((padded_ends[None, :] <= block_rows[:, None]).astype(jnp.int32), axis=1),
            N_EXPERTS - 1).astype(jnp.int32)
        pad_start = jnp.concatenate([starts + counts, padded_ends[-1:]]).astype(jnp.int32)
        pad_len = jnp.concatenate([padded - counts, n_rows - padded_ends[-1:]]).astype(jnp.int32)

        expert_ids = jnp.arange(N_EXPERTS, dtype=jnp.int32)[:, None, None]
        slots = route[TOP_K:] + jnp.sum(
            jnp.where(route[None, :TOP_K] == expert_ids, starts[:, None, None], 0), axis=0)

        xs = _dispatch(pad_start, pad_len, slots, x1, n_rows, _tile(n, 1024))
        rows = _experts(block_expert, n_used, xs,
                        w_gate_up[l], b_gate_up[l][:, None, :],
                        w_down[l], b_down[l][:, None, :])
        xcur = _combine(slots, x1, topw, ln2_g[l].reshape(1, d),
                        ln2_b[l].reshape(1, d), rows, _tile(n, 512))
    return xcur.reshape(b, s, d)
```

```python
import functools

import numpy as np
import jax
import jax.numpy as jnp
from jax import lax
from jax.experimental import pallas as pl
from jax.experimental.pallas import tpu as pltpu

F32 = jnp.float32
BF16 = jnp.bfloat16

SB_HEADS = 16
SB_HEAD_DIM = 64
RET_HEADS = 4
RET_QK_DIM = 256
RET_V_DIM = 512
RET_CHUNK = 64
ROPE_BASE = 10000.0
N_EXPERTS = 32
TOP_K = 4
EXPERT_BLOCK = 512
SWIGLU_LIMIT = 7.0
SWIGLU_ALPHA = 1.702
LN_EPS = 1e-5
DEPTH = 1
DEEPNORM_ALPHA = (2.0 * DEPTH) ** 0.25

LANES = 128
SUBLANES = 8
VMEM_LIMIT = 56 * 1024 * 1024

EXP2_UNDERFLOW_BITS = 151.0
LOG2_E = 1.4426950408889634
MASKED_SCORE = -1e30


def _cparams(sem):
    return pltpu.CompilerParams(dimension_semantics=sem, vmem_limit_bytes=VMEM_LIMIT)


def _proj_kernel(x_ref, w_ref, cs_ref, o_ref, xb_ref):
    @pl.when(pl.program_id(1) == 0)
    def _():
        xb_ref[...] = x_ref[...].astype(BF16)

    acc = jnp.dot(xb_ref[...], w_ref[...], preferred_element_type=F32)
    o_ref[...] = (acc * cs_ref[...]).astype(o_ref.dtype)


def _in_proj(x2, w_bf, col_scale, tm, tn):
    n, d = x2.shape
    width = w_bf.shape[1]
    return pl.pallas_call(
        _proj_kernel,
        out_shape=jax.ShapeDtypeStruct((n, width), BF16),
        grid=(n // tm, width // tn),
        in_specs=[pl.BlockSpec((tm, d), lambda i, j: (i, 0)),
                  pl.BlockSpec((d, tn), lambda i, j: (0, j)),
                  pl.BlockSpec((1, tn), lambda i, j: (0, j))],
        out_specs=pl.BlockSpec((tm, tn), lambda i, j: (i, j)),
        scratch_shapes=[pltpu.VMEM((tm, d), BF16)],
        compiler_params=_cparams(("parallel", "arbitrary")),
        name="in_proj",
    )(x2, w_bf, col_scale)


def _sb_kernel(q_ref, k_ref, v_ref, o_ref, acc_ref, carry_ref, *, tq, n_heads):
    qi = pl.program_id(2)
    n_groups = n_heads // 2
    lane = lax.broadcasted_iota(jnp.int32, (tq, LANES), 1)
    low = lane < SB_HEAD_DIM
    q = q_ref[...]
    zero = jnp.zeros((tq, LANES), BF16)
    q_pairs = []
    for g in range(n_groups):
        qg = q[:, g * LANES:(g + 1) * LANES]
        q_pairs.append(jnp.concatenate([jnp.where(low, qg, zero), jnp.where(low, zero, qg)], axis=0))
    def suffix_matrix(width):
        row = lax.broadcasted_iota(jnp.int32, (width, width), 0)
        col = lax.broadcasted_iota(jnp.int32, (width, width), 1)
        return (row >= col).astype(BF16)

    wide = 2 * tq
    suffix = {tq: suffix_matrix(tq), wide: suffix_matrix(wide)}
    rows_all = lax.broadcasted_iota(jnp.int32, (n_heads * tq, tq), 0)
    cols_all = lax.broadcasted_iota(jnp.int32, (n_heads * tq, tq), 1)
    strict = cols_all < (rows_all & (tq - 1))

    acc_ref[...] = jnp.zeros_like(acc_ref)
    carry_ref[...] = jnp.zeros_like(carry_ref)

    def block(start, width, diag):
        keys = pl.ds(pl.multiple_of(start, tq), width)
        z = jnp.concatenate(
            [lax.dot_general(q_pairs[g], k_ref[keys, pl.ds(g * LANES, LANES)],
                             (((1,), (1,)), ((), ())), preferred_element_type=F32)
             for g in range(n_groups)], axis=0)
        if diag:
            z = jnp.where(strict, z, MASKED_SCORE)
        sp = jnp.maximum(z, 0.0) + jnp.log2(1.0 + jnp.exp2(-jnp.abs(z)))
        cum = jnp.dot(sp.astype(BF16), suffix[width], preferred_element_type=F32)
        carry = carry_ref[...]
        a = jnp.exp2(z - cum - carry).astype(BF16)
        pv = jnp.concatenate(
            [jnp.dot(a[g * 2 * tq:(g + 1) * 2 * tq], v_ref[keys, pl.ds(g * LANES, LANES)],
                     preferred_element_type=F32)
             for g in range(n_groups)], axis=0)
        acc_ref[...] += pv
        carry_ref[...] = carry + cum[:, 0:1]

    def alive():
        return jnp.min(carry_ref[...]) <= EXP2_UNDERFLOW_BITS

    block(qi * tq, tq, True)

    n_wide = qi // 2

    def cond(c):
        j, live = c
        return jnp.logical_and(j < n_wide, live)

    def body(c):
        j, _ = c
        block(qi * tq - wide * (j + 1), wide, False)
        return j + 1, alive()

    _, live = lax.while_loop(cond, body, (0, alive()))

    @pl.when(jnp.logical_and(live, qi % 2 == 1))
    def _():
        block(0, tq, False)
    for g in range(n_groups):
        o_ref[:, g * LANES:(g + 1) * LANES] = jnp.where(
            low, acc_ref[pl.ds(2 * g * tq, tq), :], acc_ref[pl.ds((2 * g + 1) * tq, tq), :]
        ).astype(o_ref.dtype)


def _sb_attention(proj3, tq, n_heads):
    b, s, _ = proj3.shape
    width = n_heads * SB_HEAD_DIM
    groups = SB_HEADS // n_heads
    resident = pl.Buffered(1)
    return pl.pallas_call(
        functools.partial(_sb_kernel, tq=tq, n_heads=n_heads),
        out_shape=jax.ShapeDtypeStruct((b, s, SB_HEADS * SB_HEAD_DIM), BF16),
        grid=(b, groups, s // tq),
        in_specs=[pl.BlockSpec((None, tq, width), lambda bi, g, i: (bi, i, g)),
                  pl.BlockSpec((None, s, width), lambda bi, g, i: (bi, 0, groups + g),
                               pipeline_mode=resident),
                  pl.BlockSpec((None, s, width), lambda bi, g, i: (bi, 0, 2 * groups + g),
                               pipeline_mode=resident)],
        out_specs=pl.BlockSpec((None, tq, width), lambda bi, g, i: (bi, i, g)),
        scratch_shapes=[pltpu.VMEM((n_heads * tq, LANES), F32), pltpu.VMEM((n_heads * tq, 1), F32)],
        compiler_params=_cparams(("parallel", "parallel", "arbitrary")),
        name="sb_attn",
    )(proj3, proj3, proj3)


def _ret_kernel(lg_ref, q_ref, k_ref, v_ref, g_ref, cos_ref, sin_ref, o_ref,
                state_ref, mask_ref, qdec_ref, kdec_ref, *, tb):
    h = pl.program_id(1)
    si = pl.program_id(2)
    lg = lg_ref[h]
    half = RET_QK_DIM // 2

    @pl.when(si == 0)
    def _():
        state_ref[...] = jnp.zeros_like(state_ref)
        i = lax.broadcasted_iota(jnp.int32, (tb, tb), 0)
        j = lax.broadcasted_iota(jnp.int32, (tb, tb), 1)
        dist = jnp.abs(i - j).astype(F32)
        visible = (j // RET_CHUNK) <= (i // RET_CHUNK)
        mask_ref[...] = jnp.where(visible, jnp.exp(lg * dist), 0.0)
        r = lax.broadcasted_iota(jnp.int32, (tb, 1), 0).astype(F32)
        qdec_ref[...] = jnp.exp(lg * (r + 1.0))
        kdec_ref[...] = jnp.exp(lg * (tb - 1.0 - r))

    cos = cos_ref[...]
    sin = sin_ref[...]

    def rot(t):
        t1 = t[:, :half]
        t2 = t[:, half:]
        return jnp.concatenate([t1 * cos - t2 * sin, t1 * sin + t2 * cos], axis=-1)

    qr = rot(q_ref[...].astype(F32))
    kr = rot(k_ref[...].astype(F32))
    v = v_ref[...]
    scores = lax.dot_general(qr.astype(BF16), kr.astype(BF16), (((1,), (1,)), ((), ())),
                             preferred_element_type=F32) * mask_ref[...]
    state = state_ref[...]
    y = (jnp.dot(scores.astype(BF16), v, preferred_element_type=F32)
         + jnp.dot((qr * qdec_ref[...]).astype(BF16), state.astype(BF16),
                   preferred_element_type=F32))
    kd = (kr * kdec_ref[...]).astype(BF16)
    state_ref[...] = state * jnp.exp(lg * tb) + lax.dot_general(
        kd, v, (((0,), (0,)), ((), ())), preferred_element_type=F32)

    mu = jnp.mean(y, axis=-1, keepdims=True)
    yc = y - mu
    var = jnp.mean(yc * yc, axis=-1, keepdims=True)
    yn = yc * lax.rsqrt(var + LN_EPS)
    g = g_ref[...].astype(F32)
    o_ref[...] = (g * jax.nn.sigmoid(g) * yn).astype(o_ref.dtype)


def _retention(proj3, cos, sin, tb, off_q, off_k, off_v, off_g):
    b, s, _ = proj3.shape
    log_gamma = jnp.asarray(np.log(1.0 - 2.0 ** (-5.0 - np.arange(RET_HEADS))), F32)
    qb, kb = off_q // RET_QK_DIM, off_k // RET_QK_DIM
    vb, gb = off_v // RET_V_DIM, off_g // RET_V_DIM
    half = RET_QK_DIM // 2
    grid_spec = pltpu.PrefetchScalarGridSpec(
        num_scalar_prefetch=1,
        grid=(b, RET_HEADS, s // tb),
        in_specs=[pl.BlockSpec((None, tb, RET_QK_DIM), lambda bi, h, i, lg: (bi, i, qb + h)),
                  pl.BlockSpec((None, tb, RET_QK_DIM), lambda bi, h, i, lg: (bi, i, kb + h)),
                  pl.BlockSpec((None, tb, RET_V_DIM), lambda bi, h, i, lg: (bi, i, vb + h)),
                  pl.BlockSpec((None, tb, RET_V_DIM), lambda bi, h, i, lg: (bi, i, gb + h)),
                  pl.BlockSpec((tb, half), lambda bi, h, i, lg: (i, 0)),
                  pl.BlockSpec((tb, half), lambda bi, h, i, lg: (i, 0))],
        out_specs=pl.BlockSpec((None, tb, RET_V_DIM), lambda bi, h, i, lg: (bi, i, h)),
        scratch_shapes=[pltpu.VMEM((RET_QK_DIM, RET_V_DIM), F32),
                        pltpu.VMEM((tb, tb), F32),
                        pltpu.VMEM((tb, 1), F32),
                        pltpu.VMEM((tb, 1), F32)])
    return pl.pallas_call(
        functools.partial(_ret_kernel, tb=tb),
        out_shape=jax.ShapeDtypeStruct((b, s, RET_HEADS * RET_V_DIM), BF16),
        grid_spec=grid_spec,
        compiler_params=_cparams(("parallel", "parallel", "arbitrary")),
        name="retention",
    )(log_gamma, proj3, proj3, proj3, proj3, cos, sin)


def _layer_norm(h, g, b):
    mu = jnp.mean(h, axis=-1, keepdims=True)
    hc = h - mu
    var = jnp.mean(hc * hc, axis=-1, keepdims=True)
    return hc * lax.rsqrt(var + LN_EPS) * g + b


def _merge_kernel(ysb_ref, yret_ref, gsb_ref, gret_ref, x_ref, wsb_ref, wret_ref, wout_ref,
                  g1_ref, b1_ref, wrh_ref, wrl_ref, br_ref,
                  x1_ref, route_ref, topw_ref, cnt_ref, run_ref, *, tm):
    i = pl.program_id(0)

    @pl.when(i == 0)
    def _():
        run_ref[...] = jnp.zeros_like(run_ref)

    bsb = jnp.dot(ysb_ref[...], wsb_ref[...], preferred_element_type=F32)
    bret = jnp.dot(yret_ref[...], wret_ref[...], preferred_element_type=F32)
    merged = (jax.nn.sigmoid(gsb_ref[...].astype(F32)) * bsb
              + jax.nn.sigmoid(gret_ref[...].astype(F32)) * bret)
    mix = jnp.dot(merged.astype(BF16), wout_ref[...], preferred_element_type=F32)
    x1 = _layer_norm(DEEPNORM_ALPHA * x_ref[...] + mix, g1_ref[...], b1_ref[...])
    x1_ref[...] = x1

    x_hi = x1.astype(BF16)
    x_lo = (x1 - x_hi.astype(F32)).astype(BF16)
    logits = (jnp.dot(x_hi, wrh_ref[...], preferred_element_type=F32)
              + jnp.dot(x_hi, wrl_ref[...], preferred_element_type=F32)
              + jnp.dot(x_lo, wrh_ref[...], preferred_element_type=F32)) + br_ref[...]
    lane = lax.broadcasted_iota(jnp.int32, (tm, LANES), 1)
    lane_f = lane.astype(F32)
    neg = jnp.asarray(-jnp.inf, F32)
    work = jnp.where(lane < N_EXPERTS, logits, neg)
    vals, idxs = [], []
    sel = jnp.zeros((tm, LANES), F32)
    for _ in range(TOP_K):
        m = jnp.max(work, axis=-1, keepdims=True)
        idx = jnp.min(jnp.where(work == m, lane_f, float(LANES)), axis=-1, keepdims=True)
        hit = lane_f == idx
        sel = jnp.where(hit, 1.0, sel)
        work = jnp.where(hit, neg, work)
        vals.append(m)
        idxs.append(idx)
    exps = [jnp.exp(v - vals[0]) for v in vals]
    denom = exps[0] + exps[1] + exps[2] + exps[3]
    weights = [e / denom for e in exps]

    r = lax.broadcasted_iota(jnp.int32, (tm, tm), 0)
    c = lax.broadcasted_iota(jnp.int32, (tm, tm), 1)
    before = (c < r).astype(BF16)
    rank_mat = jnp.dot(before, sel.astype(BF16), preferred_element_type=F32) + run_ref[...]
    route = jnp.zeros((tm, LANES), F32)
    topw = jnp.zeros((tm, LANES), F32)
    for k in range(TOP_K):
        rank_k = jnp.sum(jnp.where(lane_f == idxs[k], rank_mat, 0.0), axis=-1, keepdims=True)
        route = jnp.where(lane == k, idxs[k], route)
        route = jnp.where(lane == TOP_K + k, rank_k, route)
        topw = jnp.where(lane == k, weights[k], topw)
    route_ref[...] = route.T[:2 * TOP_K, :].astype(jnp.int32)
    topw_ref[...] = topw
    run = run_ref[...] + jnp.sum(sel, axis=0, keepdims=True)
    run_ref[...] = run
    cnt_ref[...] = jnp.broadcast_to(run, cnt_ref.shape).astype(jnp.int32)


def _merge_route(y_sb, y_ret, proj, x2, wsb, wret, wout, g1, b1, wr_hi, wr_lo, br, tm,
                 off_gsb, off_gret):
    n, d = x2.shape
    const = lambda i: (0, 0)
    row = lambda i: (i, 0)
    return pl.pallas_call(
        functools.partial(_merge_kernel, tm=tm),
        out_shape=(jax.ShapeDtypeStruct((n, d), F32),
                   jax.ShapeDtypeStruct((2 * TOP_K, n), jnp.int32),
                   jax.ShapeDtypeStruct((n, LANES), F32),
                   jax.ShapeDtypeStruct((8, LANES), jnp.int32)),
        grid=(n // tm,),
        in_specs=[pl.BlockSpec((tm, y_sb.shape[1]), row),
                  pl.BlockSpec((tm, y_ret.shape[1]), row),
                  pl.BlockSpec((tm, d), lambda i: (i, off_gsb // d)),
                  pl.BlockSpec((tm, d), lambda i: (i, off_gret // d)),
                  pl.BlockSpec((tm, d), row),
                  pl.BlockSpec(wsb.shape, const),
                  pl.BlockSpec(wret.shape, const),
                  pl.BlockSpec(wout.shape, const),
                  pl.BlockSpec((1, d), const),
                  pl.BlockSpec((1, d), const),
                  pl.BlockSpec(wr_hi.shape, const),
                  pl.BlockSpec(wr_lo.shape, const),
                  pl.BlockSpec((1, LANES), const)],
        out_specs=(pl.BlockSpec((tm, d), row),
                   pl.BlockSpec((2 * TOP_K, tm), lambda i: (0, i)),
                   pl.BlockSpec((tm, LANES), row),
                   pl.BlockSpec((8, LANES), const)),
        scratch_shapes=[pltpu.VMEM((1, LANES), F32)],
        compiler_params=_cparams(("arbitrary",)),
        name="merge_route",
    )(y_sb, y_ret, proj, proj, x2, wsb, wret, wout, g1, b1, wr_hi, wr_lo, br)


PAD_CHUNKS = tuple(1 << b for b in reversed(range(SUBLANES.bit_length() - 1,
                                                  EXPERT_BLOCK.bit_length() - 1)))


ISSUE_UNROLL = 4
COMBINE_GROUP = 32


def _dispatch_kernel(padstart_ref, padlen_ref, slot_ref, x_ref, xs_hbm,
                     zero_ref, sem, zsem, *, tm):
    def pad_copies(act):
        def zero_copy(off, size):
            return pltpu.make_async_copy(zero_ref.at[pl.ds(0, size)],
                                         xs_hbm.at[pl.ds(off, size)], zsem)

        def per_expert(e, c):
            n = padlen_ref[e]
            off = padstart_ref[e]
            for r in range(SUBLANES - 1):
                @pl.when(r < (n & (SUBLANES - 1)))
                def _():
                    act(zero_copy(off + r, 1))
            off = pl.multiple_of(off + (n & (SUBLANES - 1)), SUBLANES)
            for size in PAD_CHUNKS:
                @pl.when((n & size) != 0)
                def _():
                    act(zero_copy(off, size))
                off = pl.multiple_of(off + (n & size), SUBLANES)
            return c

        lax.fori_loop(0, N_EXPERTS, per_expert, 0)

        def tail_chunk(j, c):
            off = pl.multiple_of(padstart_ref[N_EXPERTS] + j * PAD_CHUNKS[0], PAD_CHUNKS[0])
            act(zero_copy(off, PAD_CHUNKS[0]))
            return c

        lax.fori_loop(0, padlen_ref[N_EXPERTS] // PAD_CHUNKS[0], tail_chunk, 0)

    @pl.when(pl.program_id(0) == 0)
    def _():
        zero_ref[...] = jnp.zeros_like(zero_ref)
        pad_copies(lambda cp: cp.start())
        pad_copies(lambda cp: cp.wait())

    def issue(t, c):
        for k in range(TOP_K):
            pltpu.make_async_copy(x_ref.at[pl.ds(t, 1)],
                                  xs_hbm.at[pl.ds(slot_ref[k, t], 1)], sem).start()
        return c

    lax.fori_loop(0, tm, issue, 0, unroll=ISSUE_UNROLL)
    for _ in range(TOP_K):
        pltpu.make_async_copy(x_ref, xs_hbm.at[pl.ds(0, tm)], sem).wait()


def _dispatch(pad_start, pad_len, slots, x1, n_rows, tm):
    n, d = x1.shape
    grid_spec = pltpu.PrefetchScalarGridSpec(
        num_scalar_prefetch=2,
        grid=(n // tm,),
        in_specs=[pl.BlockSpec((TOP_K, tm), lambda i, *_: (0, i), memory_space=pltpu.SMEM),
                  pl.BlockSpec((tm, d), lambda i, *_: (i, 0))],
        out_specs=pl.BlockSpec(memory_space=pl.ANY),
        scratch_shapes=[pltpu.VMEM((PAD_CHUNKS[0], d), x1.dtype),
                        pltpu.SemaphoreType.DMA, pltpu.SemaphoreType.DMA])
    return pl.pallas_call(
        functools.partial(_dispatch_kernel, tm=tm),
        out_shape=jax.ShapeDtypeStruct((n_rows, d), x1.dtype),
        grid_spec=grid_spec,
        compiler_params=_cparams(("arbitrary",)),
        name="dispatch",
    )(pad_start, pad_len, slots, x1)


def _expert_kernel(be_ref, nb_ref, xs_ref, wgu_ref, bgu_ref, wd_ref, bd_ref, o_ref):
    del be_ref
    i = pl.program_id(0)

    @pl.when(i < nb_ref[0])
    def _():
        hgu = jnp.dot(xs_ref[...].astype(BF16), wgu_ref[...].astype(BF16),
                      preferred_element_type=F32) + bgu_ref[...]
        d_ff = hgu.shape[1] // 2
        gate = jnp.minimum(hgu[:, :d_ff], SWIGLU_LIMIT)
        up = jnp.clip(hgu[:, d_ff:], -SWIGLU_LIMIT, SWIGLU_LIMIT)
        glu = gate * jax.nn.sigmoid(gate * SWIGLU_ALPHA)
        act = ((up + 1.0) * glu).astype(BF16)
        o_ref[...] = jnp.dot(act, wd_ref[...].astype(BF16),
                             preferred_element_type=F32) + bd_ref[...]

    @pl.when(i >= nb_ref[0])
    def _():
        o_ref[...] = jnp.zeros_like(o_ref)


def _experts(block_expert, n_used, xs, wgu, bgu, wd, bd):
    n_rows, w = xs.shape
    d = wd.shape[2]
    n_blocks = n_rows // EXPERT_BLOCK
    grid_spec = pltpu.PrefetchScalarGridSpec(
        num_scalar_prefetch=2,
        grid=(n_blocks,),
        in_specs=[pl.BlockSpec((EXPERT_BLOCK, w), lambda i, be, nb: (jnp.minimum(i, nb[0] - 1), 0)),
                  pl.BlockSpec((None,) + wgu.shape[1:], lambda i, be, nb: (be[i], 0, 0)),
                  pl.BlockSpec((None,) + bgu.shape[1:], lambda i, be, nb: (be[i], 0, 0)),
                  pl.BlockSpec((None,) + wd.shape[1:], lambda i, be, nb: (be[i], 0, 0)),
                  pl.BlockSpec((None,) + bd.shape[1:], lambda i, be, nb: (be[i], 0, 0))],
        out_specs=pl.BlockSpec((EXPERT_BLOCK, d), lambda i, be, nb: (i, 0)))
    return pl.pallas_call(
        _expert_kernel,
        out_shape=jax.ShapeDtypeStruct((n_rows, d), F32),
        grid_spec=grid_spec,
        compiler_params=_cparams(("arbitrary",)),
        name="experts",
    )(block_expert, n_used, xs, wgu, bgu, wd, bd)


def _combine_kernel(slot_ref, slot_next_ref,
                    x1_ref, topw_ref, g2_ref, b2_ref, rows_hbm,
                    o_ref, buf0_ref, buf1_ref, sem, *, tc):
    bufs = (buf0_ref, buf1_ref)
    i = pl.program_id(0)
    n_groups = tc // COMBINE_GROUP

    def issue_group(s_ref, sl, c):
        for u in range(COMBINE_GROUP):
            t = c * COMBINE_GROUP + u
            for k in range(TOP_K):
                pltpu.make_async_copy(rows_hbm.at[pl.ds(s_ref[k, t], 1)],
                                      bufs[sl].at[k, pl.ds(t, 1)], sem.at[sl]).start()

    @pl.when(i == 0)
    def _():
        lax.fori_loop(0, n_groups, lambda c, a: (issue_group(slot_ref, 0, c), a)[1], 0)

    def step(slot):
        for k in range(TOP_K):
            pltpu.make_async_copy(rows_hbm.at[pl.ds(0, tc)], bufs[slot].at[k],
                                  sem.at[slot]).wait()

        def reduce_group(c):
            r = pl.ds(pl.multiple_of(c * COMBINE_GROUP, COMBINE_GROUP), COMBINE_GROUP)
            topw = topw_ref[r, :]
            y = topw[:, 0:1] * bufs[slot][0, r, :]
            for k in range(1, TOP_K):
                y = y + topw[:, k:k + 1] * bufs[slot][k, r, :]
            o_ref[r, :] = _layer_norm(DEEPNORM_ALPHA * x1_ref[r, :] + y, g2_ref[...], b2_ref[...])

        @pl.when(i + 1 < pl.num_programs(0))
        def _():
            def body(c, a):
                issue_group(slot_next_ref, 1 - slot, c)
                reduce_group(c)
                return a

            lax.fori_loop(0, n_groups, body, 0)

        @pl.when(i + 1 >= pl.num_programs(0))
        def _():
            lax.fori_loop(0, n_groups, lambda c, a: (reduce_group(c), a)[1], 0)

    for parity in range(2):
        @pl.when(i % 2 == parity)
        def _():
            step(parity)


def _combine(slots, x1, topw, g2, b2, rows, tc):
    n, d = x1.shape
    nt = n // tc
    return pl.pallas_call(
        functools.partial(_combine_kernel, tc=tc),
        out_shape=jax.ShapeDtypeStruct((n, d), F32),
        grid=(nt,),
        in_specs=[pl.BlockSpec((TOP_K, tc), lambda i: (0, i), memory_space=pltpu.SMEM),
                  pl.BlockSpec((TOP_K, tc), lambda i: (0, jnp.minimum(i + 1, nt - 1)),
                               memory_space=pltpu.SMEM),
                  pl.BlockSpec((tc, d), lambda i: (i, 0)),
                  pl.BlockSpec((tc, LANES), lambda i: (i, 0)),
                  pl.BlockSpec((1, d), lambda i: (0, 0)),
                  pl.BlockSpec((1, d), lambda i: (0, 0)),
                  pl.BlockSpec(memory_space=pl.ANY)],
        out_specs=pl.BlockSpec((tc, d), lambda i: (i, 0)),
        scratch_shapes=[pltpu.VMEM((TOP_K, tc, d), F32), pltpu.VMEM((TOP_K, tc, d), F32),
                        pltpu.SemaphoreType.DMA((2,))],
        compiler_params=_cparams(("arbitrary",)),
        name="combine",
    )(slots, slots, x1, topw, g2, b2, rows)


def _tile(n, pref):
    t = min(n, pref)
    assert n % t == 0, (n, t)
    return t


def kernel(x, w_in, w_branch_sb, w_branch_ret, w_out, ln1_g, ln1_b, w_router, b_router,
           w_gate_up, b_gate_up, w_down, b_down, ln2_g, ln2_b):
    b, s, d = x.shape
    n = b * s
    sb_w = SB_HEADS * SB_HEAD_DIM
    qk_w = RET_HEADS * RET_QK_DIM
    v_w = RET_HEADS * RET_V_DIM
    sizes = (sb_w, sb_w, sb_w, qk_w, qk_w, v_w, v_w, d, d)
    offs = [int(o) for o in np.cumsum((0,) + sizes)]
    assert w_in.shape[0] == DEPTH and w_in.shape[2] == offs[-1]
    assert d == sb_w and d % (2 * LANES) == 0

    half = RET_QK_DIM // 2
    inv_freq = ROPE_BASE ** (-jnp.arange(half, dtype=F32) / half)
    ang = jnp.arange(s, dtype=F32)[:, None] * inv_freq[None, :]
    cos, sin = jnp.cos(ang), jnp.sin(ang)

    n_assign = n * TOP_K
    n_rows = -(-n_assign // EXPERT_BLOCK) * EXPERT_BLOCK + N_EXPERTS * EXPERT_BLOCK
    n_blocks = n_rows // EXPERT_BLOCK

    col_scale = (jnp.ones((1, offs[-1]), F32).at[:, :sb_w].set(SB_HEAD_DIM ** -0.5 * LOG2_E)
                 .at[:, offs[3]:offs[4]].set(RET_QK_DIM ** -0.5))

    xcur = x.reshape(n, d)
    for l in range(DEPTH):
        proj = _in_proj(xcur, w_in[l].astype(BF16), col_scale, _tile(n, 1024), offs[-1] // 4)
        proj3 = proj.reshape(b, s, offs[-1])
        y_sb = _sb_attention(proj3, _tile(s, 128), SB_HEADS)
        y_ret = _retention(proj3, cos, sin, _tile(s, 512), offs[3], offs[4], offs[5], offs[6])

        wr = jnp.zeros((d, LANES), F32).at[:, :N_EXPERTS].set(w_router[l])
        br = jnp.zeros((1, LANES), F32).at[0, :N_EXPERTS].set(b_router[l])
        wr_hi = wr.astype(BF16)
        wr_lo = (wr - wr_hi.astype(F32)).astype(BF16)
        x1, route, topw, cnt = _merge_route(
            y_sb.reshape(n, sb_w), y_ret.reshape(n, v_w), proj, xcur,
            w_branch_sb[l].astype(BF16), w_branch_ret[l].astype(BF16), w_out[l].astype(BF16),
            ln1_g[l].reshape(1, d), ln1_b[l].reshape(1, d), wr_hi, wr_lo, br,
            _tile(n, 512), offs[7], offs[8])

        counts = cnt[0, :N_EXPERTS]
        padded = (counts + EXPERT_BLOCK - 1) // EXPERT_BLOCK * EXPERT_BLOCK
        padded_ends = jnp.cumsum(padded)
        starts = (padded_ends - padded).astype(jnp.int32)
        n_used = (padded_ends[-1:] // EXPERT_BLOCK).astype(jnp.int32)
        block_rows = jnp.arange(n_blocks, dtype=jnp.int32) * EXPERT_BLOCK
        block_expert = jnp.minimum(
            jnp.sum((padded_ends[None, :] <= block_rows[:, None]).astype(jnp.int32), axis=1),
            N_EXPERTS - 1).astype(jnp.int32)
        pad_start = jnp.concatenate([starts + counts, padded_ends[-1:]]).astype(jnp.int32)
        pad_len = jnp.concatenate([padded - counts, n_rows - padded_ends[-1:]]).astype(jnp.int32)

        expert_ids = jnp.arange(N_EXPERTS, dtype=jnp.int32)[:, None, None]
        slots = route[TOP_K:] + jnp.sum(
            jnp.where(route[None, :TOP_K] == expert_ids, starts[:, None, None], 0), axis=0)

        xs = _dispatch(pad_start, pad_len, slots, x1, n_rows, _tile(n, 1024))
        rows = _experts(block_expert, n_used, xs,
                        w_gate_up[l], b_gate_up[l][:, None, :],
                        w_down[l], b_down[l][:, None, :])
        xcur = _combine(slots, x1, topw, ln2_g[l].reshape(1, d),
                        ln2_b[l].reshape(1, d), rows, _tile(n, 512))
    return xcur.reshape(b, s, d)
```

```python
import functools

import numpy as np
import jax
import jax.numpy as jnp
from jax import lax
from jax.experimental import pallas as pl
from jax.experimental.pallas import tpu as pltpu

F32 = jnp.float32
BF16 = jnp.bfloat16

SB_HEADS = 16
SB_HEAD_DIM = 64
RET_HEADS = 4
RET_QK_DIM = 256
RET_V_DIM = 512
RET_CHUNK = 64
ROPE_BASE = 10000.0
N_EXPERTS = 32
TOP_K = 4
EXPERT_BLOCK = 512
SWIGLU_LIMIT = 7.0
SWIGLU_ALPHA = 1.702
LN_EPS = 1e-5
DEPTH = 1
DEEPNORM_ALPHA = (2.0 * DEPTH) ** 0.25

LANES = 128
SUBLANES = 8
VMEM_LIMIT = 56 * 1024 * 1024

EXP2_UNDERFLOW_BITS = 151.0
LOG2_E = 1.4426950408889634
MASKED_SCORE = -1e30


def _cparams(sem):
    return pltpu.CompilerParams(dimension_semantics=sem, vmem_limit_bytes=VMEM_LIMIT)


def _proj_kernel(x_ref, w_ref, cs_ref, o_ref, xb_ref):
    @pl.when(pl.program_id(1) == 0)
    def _():
        xb_ref[...] = x_ref[...].astype(BF16)

    acc = jnp.dot(xb_ref[...], w_ref[...], preferred_element_type=F32)
    o_ref[...] = (acc * cs_ref[...]).astype(o_ref.dtype)


def _in_proj(x2, w_bf, col_scale, tm, tn):
    n, d = x2.shape
    width = w_bf.shape[1]
    return pl.pallas_call(
        _proj_kernel,
        out_shape=jax.ShapeDtypeStruct((n, width), BF16),
        grid=(n // tm, width // tn),
        in_specs=[pl.BlockSpec((tm, d), lambda i, j: (i, 0)),
                  pl.BlockSpec((d, tn), lambda i, j: (0, j)),
                  pl.BlockSpec((1, tn), lambda i, j: (0, j))],
        out_specs=pl.BlockSpec((tm, tn), lambda i, j: (i, j)),
        scratch_shapes=[pltpu.VMEM((tm, d), BF16)],
        compiler_params=_cparams(("parallel", "arbitrary")),
        name="in_proj",
    )(x2, w_bf, col_scale)


def _sb_kernel(q_ref, k_ref, v_ref, o_ref, acc_ref, carry_ref, *, tq, n_heads):
    qi = pl.program_id(2)
    n_groups = n_heads // 2
    lane = lax.broadcasted_iota(jnp.int32, (tq, LANES), 1)
    low = lane < SB_HEAD_DIM
    q = q_ref[...]
    zero = jnp.zeros((tq, LANES), BF16)
    q_pairs = []
    for g in range(n_groups):
        qg = q[:, g * LANES:(g + 1) * LANES]
        q_pairs.append(jnp.concatenate([jnp.where(low, qg, zero), jnp.where(low, zero, qg)], axis=0))
    def suffix_matrix(width):
        row = lax.broadcasted_iota(jnp.int32, (width, width), 0)
        col = lax.broadcasted_iota(jnp.int32, (width, width), 1)
        return (row >= col).astype(BF16)

    wide = 2 * tq
    suffix = {tq: suffix_matrix(tq), wide: suffix_matrix(wide)}
    rows_all = lax.broadcasted_iota(jnp.int32, (n_heads * tq, tq), 0)
    cols_all = lax.broadcasted_iota(jnp.int32, (n_heads * tq, tq), 1)
    strict = cols_all < (rows_all & (tq - 1))

    acc_ref[...] = jnp.zeros_like(acc_ref)
    carry_ref[...] = jnp.zeros_like(carry_ref)

    def block(start, width, diag):
        keys = pl.ds(pl.multiple_of(start, tq), width)
        z = jnp.concatenate(
            [lax.dot_general(q_pairs[g], k_ref[keys, pl.ds(g * LANES, LANES)],
                             (((1,), (1,)), ((), ())), preferred_element_type=F32)
             for g in range(n_groups)], axis=0)
        if diag:
            z = jnp.where(strict, z, MASKED_SCORE)
        sp = jnp.maximum(z, 0.0) + jnp.log2(1.0 + jnp.exp2(-jnp.abs(z)))
        cum = jnp.dot(sp.astype(BF16), suffix[width], preferred_element_type=F32)
        carry = carry_ref[...]
        a = jnp.exp2(z - cum - carry).astype(BF16)
        pv = jnp.concatenate(
            [jnp.dot(a[g * 2 * tq:(g + 1) * 2 * tq], v_ref[keys, pl.ds(g * LANES, LANES)],
                     preferred_element_type=F32)
             for g in range(n_groups)], axis=0)
        acc_ref[...] += pv
        carry_ref[...] = carry + cum[:, 0:1]

    def alive():
        return jnp.min(carry_ref[...]) <= EXP2_UNDERFLOW_BITS

    block(qi * tq, tq, True)

    n_wide = qi // 2

    def cond(c):
        j, live = c
        return jnp.logical_and(j < n_wide, live)

    def body(c):
        j, _ = c
        block(qi * tq - wide * (j + 1), wide, False)
        return j + 1, alive()

    _, live = lax.while_loop(cond, body, (0, alive()))

    @pl.when(jnp.logical_and(live, qi % 2 == 1))
    def _():
        block(0, tq, False)
    for g in range(n_groups):
        o_ref[:, g * LANES:(g + 1) * LANES] = jnp.where(
            low, acc_ref[pl.ds(2 * g * tq, tq), :], acc_ref[pl.ds((2 * g + 1) * tq, tq), :]
        ).astype(o_ref.dtype)


def _sb_attention(proj3, tq, n_heads):
    b, s, _ = proj3.shape
    width = n_heads * SB_HEAD_DIM
    groups = SB_HEADS // n_heads
    resident = pl.Buffered(1)
    return pl.pallas_call(
        functools.partial(_sb_kernel, tq=tq, n_heads=n_heads),
        out_shape=jax.ShapeDtypeStruct((b, s, SB_HEADS * SB_HEAD_DIM), BF16),
        grid=(b, groups, s // tq),
        in_specs=[pl.BlockSpec((None, tq, width), lambda bi, g, i: (bi, i, g)),
                  pl.BlockSpec((None, s, width), lambda bi, g, i: (bi, 0, groups + g),
                               pipeline_mode=resident),
                  pl.BlockSpec((None, s, width), lambda bi, g, i: (bi, 0, 2 * groups + g),
                               pipeline_mode=resident)],
        out_specs=pl.BlockSpec((None, tq, width), lambda bi, g, i: (bi, i, g)),
        scratch_shapes=[pltpu.VMEM((n_heads * tq, LANES), F32), pltpu.VMEM((n_heads * tq, 1), F32)],
        compiler_params=_cparams(("parallel", "parallel", "arbitrary")),
        name="sb_attn",
    )(proj3, proj3, proj3)


def _ret_kernel(lg_ref, q_ref, k_ref, v_ref, g_ref, cos_ref, sin_ref, o_ref,
                state_ref, mask_ref, qdec_ref, kdec_ref, *, tb):
    h = pl.program_id(1)
    si = pl.program_id(2)
    lg = lg_ref[h]
    half = RET_QK_DIM // 2

    @pl.when(si == 0)
    def _():
        state_ref[...] = jnp.zeros_like(state_ref)
        i = lax.broadcasted_iota(jnp.int32, (tb, tb), 0)
        j = lax.broadcasted_iota(jnp.int32, (tb, tb), 1)
        dist = jnp.abs(i - j).astype(F32)
        visible = (j // RET_CHUNK) <= (i // RET_CHUNK)
        mask_ref[...] = jnp.where(visible, jnp.exp(lg * dist), 0.0)
        r = lax.broadcasted_iota(jnp.int32, (tb, 1), 0).astype(F32)
        qdec_ref[...] = jnp.exp(lg * (r + 1.0))
        kdec_ref[...] = jnp.exp(lg * (tb - 1.0 - r))

    cos = cos_ref[...]
    sin = sin_ref[...]

    def rot(t):
        t1 = t[:, :half]
        t2 = t[:, half:]
        return jnp.concatenate([t1 * cos - t2 * sin, t1 * sin + t2 * cos], axis=-1)

    qr = rot(q_ref[...].astype(F32))
    kr = rot(k_ref[...].astype(F32))
    v = v_ref[...]
    scores = lax.dot_general(qr.astype(BF16), kr.astype(BF16), (((1,), (1,)), ((), ())),
                             preferred_element_type=F32) * mask_ref[...]
    state = state_ref[...]
    y = (jnp.dot(scores.astype(BF16), v, preferred_element_type=F32)
         + jnp.dot((qr * qdec_ref[...]).astype(BF16), state.astype(BF16),
                   preferred_element_type=F32))
    kd = (kr * kdec_ref[...]).astype(BF16)
    state_ref[...] = state * jnp.exp(lg * tb) + lax.dot_general(
        kd, v, (((0,), (0,)), ((), ())), preferred_element_type=F32)

    mu = jnp.mean(y, axis=-1, keepdims=True)
    yc = y - mu
    var = jnp.mean(yc * yc, axis=-1, keepdims=True)
    yn = yc * lax.rsqrt(var + LN_EPS)
    g = g_ref[...].astype(F32)
    o_ref[...] = (g * jax.nn.sigmoid(g) * yn).astype(o_ref.dtype)


def _retention(proj3, cos, sin, tb, off_q, off_k, off_v, off_g):
    b, s, _ = proj3.shape
    log_gamma = jnp.asarray(np.log(1.0 - 2.0 ** (-5.0 - np.arange(RET_HEADS))), F32)
    qb, kb = off_q // RET_QK_DIM, off_k // RET_QK_DIM
    vb, gb = off_v // RET_V_DIM, off_g // RET_V_DIM
    half = RET_QK_DIM // 2
    grid_spec = pltpu.PrefetchScalarGridSpec(
        num_scalar_prefetch=1,
        grid=(b, RET_HEADS, s // tb),
        in_specs=[pl.BlockSpec((None, tb, RET_QK_DIM), lambda bi, h, i, lg: (bi, i, qb + h)),
                  pl.BlockSpec((None, tb, RET_QK_DIM), lambda bi, h, i, lg: (bi, i, kb + h)),
                  pl.BlockSpec((None, tb, RET_V_DIM), lambda bi, h, i, lg: (bi, i, vb + h)),
                  pl.BlockSpec((None, tb, RET_V_DIM), lambda bi, h, i, lg: (bi, i, gb + h)),
                  pl.BlockSpec((tb, half), lambda bi, h, i, lg: (i, 0)),
                  pl.BlockSpec((tb, half), lambda bi, h, i, lg: (i, 0))],
        out_specs=pl.BlockSpec((None, tb, RET_V_DIM), lambda bi, h, i, lg: (bi, i, h)),
        scratch_shapes=[pltpu.VMEM((RET_QK_DIM, RET_V_DIM), F32),
                        pltpu.VMEM((tb, tb), F32),
                        pltpu.VMEM((tb, 1), F32),
                        pltpu.VMEM((tb, 1), F32)])
    return pl.pallas_call(
        functools.partial(_ret_kernel, tb=tb),
        out_shape=jax.ShapeDtypeStruct((b, s, RET_HEADS * RET_V_DIM), BF16),
        grid_spec=grid_spec,
        compiler_params=_cparams(("parallel", "parallel", "arbitrary")),
        name="retention",
    )(log_gamma, proj3, proj3, proj3, proj3, cos, sin)


def _layer_norm(h, g, b):
    mu = jnp.mean(h, axis=-1, keepdims=True)
    hc = h - mu
    var = jnp.mean(hc * hc, axis=-1, keepdims=True)
    return hc * lax.rsqrt(var + LN_EPS) * g + b


def _merge_kernel(ysb_ref, yret_ref, gsb_ref, gret_ref, x_ref, wsb_ref, wret_ref, wout_ref,
                  g1_ref, b1_ref, wrh_ref, wrl_ref, br_ref,
                  x1_ref, route_ref, topw_ref, cnt_ref, run_ref, *, tm):
    i = pl.program_id(0)

    @pl.when(i == 0)
    def _():
        run_ref[...] = jnp.zeros_like(run_ref)

    bsb = jnp.dot(ysb_ref[...], wsb_ref[...], preferred_element_type=F32)
    bret = jnp.dot(yret_ref[...], wret_ref[...], preferred_element_type=F32)
    merged = (jax.nn.sigmoid(gsb_ref[...].astype(F32)) * bsb
              + jax.nn.sigmoid(gret_ref[...].astype(F32)) * bret)
    mix = jnp.dot(merged.astype(BF16), wout_ref[...], preferred_element_type=F32)
    x1 = _layer_norm(DEEPNORM_ALPHA * x_ref[...] + mix, g1_ref[...], b1_ref[...])
    x1_ref[...] = x1

    x_hi = x1.astype(BF16)
    x_lo = (x1 - x_hi.astype(F32)).astype(BF16)
    logits = (jnp.dot(x_hi, wrh_ref[...], preferred_element_type=F32)
              + jnp.dot(x_hi, wrl_ref[...], preferred_element_type=F32)
              + jnp.dot(x_lo, wrh_ref[...], preferred_element_type=F32)) + br_ref[...]
    lane = lax.broadcasted_iota(jnp.int32, (tm, LANES), 1)
    lane_f = lane.astype(F32)
    neg = jnp.asarray(-jnp.inf, F32)
    work = jnp.where(lane < N_EXPERTS, logits, neg)
    vals, idxs = [], []
    sel = jnp.zeros((tm, LANES), F32)
    for _ in range(TOP_K):
        m = jnp.max(work, axis=-1, keepdims=True)
        idx = jnp.min(jnp.where(work == m, lane_f, float(LANES)), axis=-1, keepdims=True)
        hit = lane_f == idx
        sel = jnp.where(hit, 1.0, sel)
        work = jnp.where(hit, neg, work)
        vals.append(m)
        idxs.append(idx)
    exps = [jnp.exp(v - vals[0]) for v in vals]
    denom = exps[0] + exps[1] + exps[2] + exps[3]
    weights = [e / denom for e in exps]

    r = lax.broadcasted_iota(jnp.int32, (tm, tm), 0)
    c = lax.broadcasted_iota(jnp.int32, (tm, tm), 1)
    before = (c < r).astype(BF16)
    rank_mat = jnp.dot(before, sel.astype(BF16), preferred_element_type=F32) + run_ref[...]
    route = jnp.zeros((tm, LANES), F32)
    topw = jnp.zeros((tm, LANES), F32)
    for k in range(TOP_K):
        rank_k = jnp.sum(jnp.where(lane_f == idxs[k], rank_mat, 0.0), axis=-1, keepdims=True)
        route = jnp.where(lane == k, idxs[k], route)
        route = jnp.where(lane == TOP_K + k, rank_k, route)
        topw = jnp.where(lane == k, weights[k], topw)
    route_ref[...] = route.T[:2 * TOP_K, :].astype(jnp.int32)
    topw_ref[...] = topw
    run = run_ref[...] + jnp.sum(sel, axis=0, keepdims=True)
    run_ref[...] = run
    cnt_ref[...] = jnp.broadcast_to(run, cnt_ref.shape).astype(jnp.int32)


def _merge_route(y_sb, y_ret, proj, x2, wsb, wret, wout, g1, b1, wr_hi, wr_lo, br, tm,
                 off_gsb, off_gret):
    n, d = x2.shape
    const = lambda i: (0, 0)
    row = lambda i: (i, 0)
    return pl.pallas_call(
        functools.partial(_merge_kernel, tm=tm),
        out_shape=(jax.ShapeDtypeStruct((n, d), F32),
                   jax.ShapeDtypeStruct((2 * TOP_K, n), jnp.int32),
                   jax.ShapeDtypeStruct((n, LANES), F32),
                   jax.ShapeDtypeStruct((8, LANES), jnp.int32)),
        grid=(n // tm,),
        in_specs=[pl.BlockSpec((tm, y_sb.shape[1]), row),
                  pl.BlockSpec((tm, y_ret.shape[1]), row),
                  pl.BlockSpec((tm, d), lambda i: (i, off_gsb // d)),
                  pl.BlockSpec((tm, d), lambda i: (i, off_gret // d)),
                  pl.BlockSpec((tm, d), row),
                  pl.BlockSpec(wsb.shape, const),
                  pl.BlockSpec(wret.shape, const),
                  pl.BlockSpec(wout.shape, const),
                  pl.BlockSpec((1, d), const),
                  pl.BlockSpec((1, d), const),
                  pl.BlockSpec(wr_hi.shape, const),
                  pl.BlockSpec(wr_lo.shape, const),
                  pl.BlockSpec((1, LANES), const)],
        out_specs=(pl.BlockSpec((tm, d), row),
                   pl.BlockSpec((2 * TOP_K, tm), lambda i: (0, i)),
                   pl.BlockSpec((tm, LANES), row),
                   pl.BlockSpec((8, LANES), const)),
        scratch_shapes=[pltpu.VMEM((1, LANES), F32)],
        compiler_params=_cparams(("arbitrary",)),
        name="merge_route",
    )(y_sb, y_ret, proj, proj, x2, wsb, wret, wout, g1, b1, wr_hi, wr_lo, br)


PAD_CHUNKS = tuple(1 << b for b in reversed(range(SUBLANES.bit_length() - 1,
                                                  EXPERT_BLOCK.bit_length() - 1)))


ISSUE_UNROLL = 8
DMA_PRIORITIES = 2


def _dispatch_kernel(padstart_ref, padlen_ref, slot_ref, x_ref, xs_hbm,
                     zero_ref, sem, zsem, *, tm):
    def pad_copies(act):
        def zero_copy(off, size):
            return pltpu.make_async_copy(zero_ref.at[pl.ds(0, size)],
                                         xs_hbm.at[pl.ds(off, size)], zsem)

        def per_expert(e, c):
            n = padlen_ref[e]
            off = padstart_ref[e]
            for r in range(SUBLANES - 1):
                @pl.when(r < (n & (SUBLANES - 1)))
                def _():
                    act(zero_copy(off + r, 1))
            off = pl.multiple_of(off + (n & (SUBLANES - 1)), SUBLANES)
            for size in PAD_CHUNKS:
                @pl.when((n & size) != 0)
                def _():
                    act(zero_copy(off, size))
                off = pl.multiple_of(off + (n & size), SUBLANES)
            return c

        lax.fori_loop(0, N_EXPERTS, per_expert, 0)

        def tail_chunk(j, c):
            off = pl.multiple_of(padstart_ref[N_EXPERTS] + j * PAD_CHUNKS[0], PAD_CHUNKS[0])
            act(zero_copy(off, PAD_CHUNKS[0]))
            return c

        lax.fori_loop(0, padlen_ref[N_EXPERTS] // PAD_CHUNKS[0], tail_chunk, 0)

    @pl.when(pl.program_id(0) == 0)
    def _():
        zero_ref[...] = jnp.zeros_like(zero_ref)
        pad_copies(lambda cp: cp.start())
        pad_copies(lambda cp: cp.wait())

    def issue(t, c):
        for k in range(TOP_K):
            pltpu.make_async_copy(x_ref.at[pl.ds(t, 1)],
                                  xs_hbm.at[pl.ds(slot_ref[k, t], 1)], sem
                                  ).start(priority=k % DMA_PRIORITIES)
        return c

    lax.fori_loop(0, tm, issue, 0, unroll=ISSUE_UNROLL)
    for _ in range(TOP_K):
        pltpu.make_async_copy(x_ref, xs_hbm.at[pl.ds(0, tm)], sem).wait()


def _dispatch(pad_start, pad_len, slots, x1, n_rows, tm):
    n, d = x1.shape
    grid_spec = pltpu.PrefetchScalarGridSpec(
        num_scalar_prefetch=2,
        grid=(n // tm,),
        in_specs=[pl.BlockSpec((TOP_K, tm), lambda i, *_: (0, i), memory_space=pltpu.SMEM),
                  pl.BlockSpec((tm, d), lambda i, *_: (i, 0))],
        out_specs=pl.BlockSpec(memory_space=pl.ANY),
        scratch_shapes=[pltpu.VMEM((PAD_CHUNKS[0], d), x1.dtype),
                        pltpu.SemaphoreType.DMA, pltpu.SemaphoreType.DMA])
    return pl.pallas_call(
        functools.partial(_dispatch_kernel, tm=tm),
        out_shape=jax.ShapeDtypeStruct((n_rows, d), x1.dtype),
        grid_spec=grid_spec,
        compiler_params=_cparams(("arbitrary",)),
        name="dispatch",
    )(pad_start, pad_len, slots, x1)


def _expert_kernel(be_ref, nb_ref, xs_ref, wgu_ref, bgu_ref, wd_ref, bd_ref, o_ref):
    del be_ref
    i = pl.program_id(0)

    @pl.when(i < nb_ref[0])
    def _():
        hgu = jnp.dot(xs_ref[...].astype(BF16), wgu_ref[...].astype(BF16),
                      preferred_element_type=F32) + bgu_ref[...]
        d_ff = hgu.shape[1] // 2
        gate = jnp.minimum(hgu[:, :d_ff], SWIGLU_LIMIT)
        up = jnp.clip(hgu[:, d_ff:], -SWIGLU_LIMIT, SWIGLU_LIMIT)
        glu = gate * jax.nn.sigmoid(gate * SWIGLU_ALPHA)
        act = ((up + 1.0) * glu).astype(BF16)
        o_ref[...] = jnp.dot(act, wd_ref[...].astype(BF16),
                             preferred_element_type=F32) + bd_ref[...]

    @pl.when(i >= nb_ref[0])
    def _():
        o_ref[...] = jnp.zeros_like(o_ref)


def _experts(block_expert, n_used, xs, wgu, bgu, wd, bd):
    n_rows, w = xs.shape
    d = wd.shape[2]
    n_blocks = n_rows // EXPERT_BLOCK
    grid_spec = pltpu.PrefetchScalarGridSpec(
        num_scalar_prefetch=2,
        grid=(n_blocks,),
        in_specs=[pl.BlockSpec((EXPERT_BLOCK, w), lambda i, be, nb: (jnp.minimum(i, nb[0] - 1), 0)),
                  pl.BlockSpec((None,) + wgu.shape[1:], lambda i, be, nb: (be[i], 0, 0)),
                  pl.BlockSpec((None,) + bgu.shape[1:], lambda i, be, nb: (be[i], 0, 0)),
                  pl.BlockSpec((None,) + wd.shape[1:], lambda i, be, nb: (be[i], 0, 0)),
                  pl.BlockSpec((None,) + bd.shape[1:], lambda i, be, nb: (be[i], 0, 0))],
        out_specs=pl.BlockSpec((EXPERT_BLOCK, d), lambda i, be, nb: (i, 0)))
    return pl.pallas_call(
        _expert_kernel,
        out_shape=jax.ShapeDtypeStruct((n_rows, d), F32),
        grid_spec=grid_spec,
        compiler_params=_cparams(("arbitrary",)),
        name="experts",
    )(block_expert, n_used, xs, wgu, bgu, wd, bd)


def _combine_kernel(slot_ref, slot_next_ref,
                    x1_ref, topw_ref, g2_ref, b2_ref, rows_hbm,
                    o_ref, buf_ref, sem, *, tc):
    i = pl.program_id(0)

    def issue_tile(s_ref, sl):
        def issue(t, c):
            for k in range(TOP_K):
                pltpu.make_async_copy(rows_hbm.at[pl.ds(s_ref[k, t], 1)],
                                      buf_ref.at[sl, k, pl.ds(t, 1)], sem.at[sl]
                                      ).start(priority=k % DMA_PRIORITIES)
            return c

        lax.fori_loop(0, tc, issue, 0, unroll=ISSUE_UNROLL)

    @pl.when(i == 0)
    def _():
        issue_tile(slot_ref, 0)

    def step(slot):
        @pl.when(i + 1 < pl.num_programs(0))
        def _():
            issue_tile(slot_next_ref, 1 - slot)

        for k in range(TOP_K):
            pltpu.make_async_copy(rows_hbm.at[pl.ds(0, tc)], buf_ref.at[slot, k],
                                  sem.at[slot]).wait()

        topw = topw_ref[...]
        y = topw[:, 0:1] * buf_ref[slot, 0]
        for k in range(1, TOP_K):
            y = y + topw[:, k:k + 1] * buf_ref[slot, k]
        o_ref[...] = _layer_norm(DEEPNORM_ALPHA * x1_ref[...] + y, g2_ref[...], b2_ref[...])

    for parity in range(2):
        @pl.when(i % 2 == parity)
        def _():
            step(parity)


def _combine(slots, x1, topw, g2, b2, rows, tc):
    n, d = x1.shape
    nt = n // tc
    return pl.pallas_call(
        functools.partial(_combine_kernel, tc=tc),
        out_shape=jax.ShapeDtypeStruct((n, d), F32),
        grid=(nt,),
        in_specs=[pl.BlockSpec((TOP_K, tc), lambda i: (0, i), memory_space=pltpu.SMEM),
                  pl.BlockSpec((TOP_K, tc), lambda i: (0, jnp.minimum(i + 1, nt - 1)),
                               memory_space=pltpu.SMEM),
                  pl.BlockSpec((tc, d), lambda i: (i, 0)),
                  pl.BlockSpec((tc, LANES), lambda i: (i, 0)),
                  pl.BlockSpec((1, d), lambda i: (0, 0)),
                  pl.BlockSpec((1, d), lambda i: (0, 0)),
                  pl.BlockSpec(memory_space=pl.ANY)],
        out_specs=pl.BlockSpec((tc, d), lambda i: (i, 0)),
        scratch_shapes=[pltpu.VMEM((2, TOP_K, tc, d), F32), pltpu.SemaphoreType.DMA((2,))],
        compiler_params=_cparams(("arbitrary",)),
        name="combine",
    )(slots, slots, x1, topw, g2, b2, rows)


def _tile(n, pref):
    t = min(n, pref)
    assert n % t == 0, (n, t)
    return t


def kernel(x, w_in, w_branch_sb, w_branch_ret, w_out, ln1_g, ln1_b, w_router, b_router,
           w_gate_up, b_gate_up, w_down, b_down, ln2_g, ln2_b):
    b, s, d = x.shape
    n = b * s
    sb_w = SB_HEADS * SB_HEAD_DIM
    qk_w = RET_HEADS * RET_QK_DIM
    v_w = RET_HEADS * RET_V_DIM
    sizes = (sb_w, sb_w, sb_w, qk_w, qk_w, v_w, v_w, d, d)
    offs = [int(o) for o in np.cumsum((0,) + sizes)]
    assert w_in.shape[0] == DEPTH and w_in.shape[2] == offs[-1]
    assert d == sb_w and d % (2 * LANES) == 0

    half = RET_QK_DIM // 2
    inv_freq = ROPE_BASE ** (-jnp.arange(half, dtype=F32) / half)
    ang = jnp.arange(s, dtype=F32)[:, None] * inv_freq[None, :]
    cos, sin = jnp.cos(ang), jnp.sin(ang)

    n_assign = n * TOP_K
    n_rows = -(-n_assign // EXPERT_BLOCK) * EXPERT_BLOCK + N_EXPERTS * EXPERT_BLOCK
    n_blocks = n_rows // EXPERT_BLOCK

    col_scale = (jnp.ones((1, offs[-1]), F32).at[:, :sb_w].set(SB_HEAD_DIM ** -0.5 * LOG2_E)
                 .at[:, offs[3]:offs[4]].set(RET_QK_DIM ** -0.5))

    xcur = x.reshape(n, d)
    for l in range(DEPTH):
        proj = _in_proj(xcur, w_in[l].astype(BF16), col_scale, _tile(n, 1024), offs[-1] // 4)
        proj3 = proj.reshape(b, s, offs[-1])
        y_sb = _sb_attention(proj3, _tile(s, 128), SB_HEADS)
        y_ret = _retention(proj3, cos, sin, _tile(s, 512), offs[3], offs[4], offs[5], offs[6])

        wr = jnp.zeros((d, LANES), F32).at[:, :N_EXPERTS].set(w_router[l])
        br = jnp.zeros((1, LANES), F32).at[0, :N_EXPERTS].set(b_router[l])
        wr_hi = wr.astype(BF16)
        wr_lo = (wr - wr_hi.astype(F32)).astype(BF16)
        x1, route, topw, cnt = _merge_route(
            y_sb.reshape(n, sb_w), y_ret.reshape(n, v_w), proj, xcur,
            w_branch_sb[l].astype(BF16), w_branch_ret[l].astype(BF16), w_out[l].astype(BF16),
            ln1_g[l].reshape(1, d), ln1_b[l].reshape(1, d), wr_hi, wr_lo, br,
            _tile(n, 512), offs[7], offs[8])

        counts = cnt[0, :N_EXPERTS]
        padded = (counts + EXPERT_BLOCK - 1) // EXPERT_BLOCK * EXPERT_BLOCK
        padded_ends = jnp.cumsum(padded)
        starts = (padded_ends - padded).astype(jnp.int32)
        n_used = (padded_ends[-1:] // EXPERT_BLOCK).astype(jnp.int32)
        block_rows = jnp.arange(n_blocks, dtype=jnp.int32) * EXPERT_BLOCK
        block_expert = jnp.minimum(
            jnp.sum((padded_ends[None, :] <= block_rows[:, None]).astype(jnp.int32), axis=1),
            N_EXPERTS - 1).astype(jnp.int32)
        pad_start = jnp.concatenate([starts + counts, padded_ends[-1:]]).astype(jnp.int32)
        pad_len = jnp.concatenate([padded - counts, n_rows - padded_ends[-1:]]).astype(jnp.int32)

        expert_ids = jnp.arange(N_EXPERTS, dtype=jnp.int32)[:, None, None]
        slots = route[TOP_K:] + jnp.sum(
            jnp.where(route[None, :TOP_K] == expert_ids, starts[:, None, None], 0), axis=0)

        xs = _dispatch(pad_start, pad_len, slots, x1, n_rows, _tile(n, 1024))
        rows = _experts(block_expert, n_used, xs,
                        w_gate_up[l], b_gate_up[l][:, None, :],
                        w_down[l], b_down[l][:, None, :])
        xcur = _combine(slots, x1, topw, ln2_g[l].reshape(1, d),
                        ln2_b[l].reshape(1, d), rows, _tile(n, 512))
    return xcur.reshape(b, s, d)
```
